```python
import functools
import jax
import jax.numpy as jnp
from jax import lax
import numpy as np

D_MODEL = 4096
BATCH = 2
SEQ = 4096
DEPTH = 2
DEC_BATCH = 16
DEC_SEQ = 64
PAST_LEN = 4096

CHUNK = 64
QBLOCK = 128
EPS = 1e-6
NEG_INF = -1e30

FOX_HEADS = 8
FOX_HEAD_DIM = 128
FOX_WIDTH = FOX_HEADS * FOX_HEAD_DIM

POOL_WINDOWS = (2, 4, 8, 16)
POOL_GROUPS = 4
POOL_WIDTH = D_MODEL // 4
POOL_GROUP_DIM = POOL_WIDTH // POOL_GROUPS
POOL_HIST = 15

RET_HEADS = 8
RET_KEY_DIM = 128
RET_VAL_DIM = 256
RET_QK_WIDTH = RET_HEADS * RET_KEY_DIM
RET_V_WIDTH = RET_HEADS * RET_VAL_DIM
ROPE_BASE = 10000.0

N_BRANCH = 3

MEM_TOKENS = 256
MEM_HEADS = 4
MEM_HEAD_DIM = 256
MEM_WIDTH = MEM_HEADS * MEM_HEAD_DIM

FFN_DIM = 11008
N_EXPERTS = 8
TOP_K = 2
EXPERT_DIM = 14336
MOE_BLOCK = 256
N_DENSE = (DEPTH + 1) // 2
N_MOE = DEPTH // 2

IN_SIZES = (FOX_WIDTH, FOX_WIDTH, FOX_WIDTH, FOX_HEADS, POOL_WIDTH,
            RET_QK_WIDTH, RET_QK_WIDTH, RET_V_WIDTH, RET_V_WIDTH, N_BRANCH * D_MODEL)
IN_WIDTH = 3 * FOX_WIDTH + FOX_HEADS + POOL_WIDTH + 2 * RET_QK_WIDTH + 2 * RET_V_WIDTH + N_BRANCH * D_MODEL

kernel_name = 'hybrid_fox_pool_retention_stream_step'


def _rmsnorm(x, g):
    xf = x.astype(jnp.float32)
    y = xf * lax.rsqrt(jnp.mean(xf * xf, axis=-1, keepdims=True) + EPS)
    return (y * g.astype(jnp.float32)).astype(x.dtype)


def _split_in(h):
    outs = []
    off = 0
    for size in IN_SIZES:
        outs.append(h[..., off:off + size])
        off += size
    return outs


def _fox_attention(q, k, v, cq, ck, q_pos, k_pos):
    b, lq, h, dh = q.shape
    blk = QBLOCK if lq % QBLOCK == 0 else lq
    nb = lq // blk
    scale = dh ** -0.5
    ck_t = jnp.swapaxes(ck, 1, 2)[:, :, None, :]

    def to_blocks(a):
        return jnp.moveaxis(a.reshape((b, nb, blk) + a.shape[2:]), 1, 0)

    def one_block(args):
        qb, cqb, pb = args
        s = jnp.einsum('bqhd,bkhd->bhqk', qb, k, preferred_element_type=jnp.float32) * scale
        s = s + jnp.swapaxes(cqb, 1, 2)[..., None] - ck_t
        s = jnp.where((k_pos[None, :] <= pb[:, None])[None, None], s, NEG_INF)
        p = jax.nn.softmax(s, axis=-1).astype(v.dtype)
        return jnp.einsum('bhqk,bkhd->bqhd', p, v)

    o = lax.map(one_block, (to_blocks(q), to_blocks(cq), q_pos.reshape(nb, blk)))
    return jnp.moveaxis(o, 0, 1).reshape(b, lq, h * dh)


def _pool_mixer(u, hist, pos, w_group, scale):
    b, l, c = u.shape
    ext = jnp.concatenate([hist.astype(u.dtype), u], axis=1)
    cs = jnp.cumsum(ext.astype(jnp.float32), axis=1)
    cs = jnp.concatenate([jnp.zeros((b, 1, c), jnp.float32), cs], axis=1)
    uf = u.astype(jnp.float32)
    groups = []
    for g, w in enumerate(POOL_WINDOWS):
        sl = slice(g * POOL_GROUP_DIM, (g + 1) * POOL_GROUP_DIM)
        hi = cs[:, POOL_HIST + 1:POOL_HIST + 1 + l, sl]
        lo = cs[:, POOL_HIST + 1 - w:POOL_HIST + 1 - w + l, sl]
        cnt = jnp.minimum(pos + 1, w).astype(jnp.float32)[None, :, None]
        groups.append((hi - lo) / cnt - uf[..., sl])
    pooled = jnp.stack(groups, axis=2).astype(u.dtype)
    y = jnp.einsum('blgc,gcd->blgd', pooled, w_group).reshape(b, l, c) * scale
    return y, ext[:, -POOL_HIST:]


def _rope(x, pos):
    half = x.shape[-1] // 2
    inv = ROPE_BASE ** (-jnp.arange(half, dtype=jnp.float32) / half)
    ang = pos.astype(jnp.float32)[:, None] * inv[None, :]
    cos = jnp.cos(ang)[None, :, None, :]
    sin = jnp.sin(ang)[None, :, None, :]
    xf = x.astype(jnp.float32)
    x1, x2 = xf[..., :half], xf[..., half:]
    return jnp.concatenate([x1 * cos - x2 * sin, x1 * sin + x2 * cos], axis=-1)


def _retention(q, k, v, state0):
    b, l, h, dk = q.shape
    dv = v.shape[-1]
    c = min(CHUNK, l)
    n = l // c
    ld = jnp.log1p(-jnp.exp2(-5.0 - jnp.arange(h, dtype=jnp.float32)))
    i = jnp.arange(c, dtype=jnp.float32)
    diff = i[:, None] - i[None, :]
    dmask = jnp.where(diff >= 0, jnp.exp(ld[:, None, None] * jnp.maximum(diff, 0.0)), 0.0)
    qc = q.reshape(b, n, c, h, dk)
    kc = k.reshape(b, n, c, h, dk)
    vc = v.reshape(b, n, c, h, dv)
    scores = jnp.einsum('bnihd,bnjhd->bnhij', qc, kc) * dmask
    o_intra = jnp.einsum('bnhij,bnjhe->bnihe', scores, vc)
    k_dec = kc * jnp.exp(ld[None, :] * (c - 1.0 - i)[:, None])[:, :, None]
    kv = jnp.einsum('bnjhd,bnjhe->bnhde', k_dec, vc)
    chunk_decay = jnp.exp(ld * c)[None, :, None, None]

    def step(s, kv_n):
        return chunk_decay * s + kv_n, s

    s_final, s_prev = lax.scan(step, state0.astype(jnp.float32), jnp.moveaxis(kv, 1, 0))
    q_dec = qc * jnp.exp(ld[None, :] * (i + 1.0)[:, None])[:, :, None]
    o_cross = jnp.einsum('bnihd,nbhde->bnihe', q_dec, s_prev)
    return (o_intra + o_cross).reshape(b, l, h, dv), s_final


def _token_mixing(xn, lw, start_pos, fox_past, pool_hist, ret_state):
    b, l, _ = xn.shape
    fq, fk, fv, f_logit, pu, rq, rk, rv, rg, g_logit = _split_in(jnp.matmul(xn, lw['w_in']))
    pos = start_pos + jnp.arange(l, dtype=jnp.int32)
    fq = fq.reshape(b, l, FOX_HEADS, FOX_HEAD_DIM)
    fk = fk.reshape(b, l, FOX_HEADS, FOX_HEAD_DIM)
    fv = fv.reshape(b, l, FOX_HEADS, FOX_HEAD_DIM)
    logf = jax.nn.log_sigmoid(f_logit.astype(jnp.float32) + lw['b_forget'].astype(jnp.float32))
    if fox_past is None:
        k_all, v_all, logf_all = fk, fv, logf
    else:
        pk, pv, plogf = fox_past
        k_all = jnp.concatenate([pk.astype(fk.dtype), fk], axis=1)
        v_all = jnp.concatenate([pv.astype(fv.dtype), fv], axis=1)
        logf_all = jnp.concatenate([plogf.astype(jnp.float32), logf], axis=1)
    n_past = k_all.shape[1] - l
    csum = jnp.cumsum(logf_all, axis=1)
    k_pos = (start_pos - n_past) + jnp.arange(n_past + l, dtype=jnp.int32)
    fo = _fox_attention(fq, k_all, v_all, csum[:, n_past:], csum, pos, k_pos)
    po, new_hist = _pool_mixer(pu, pool_hist, pos, lw['w_pool'], lw['pool_scale'])
    rqh = _rope(rq.reshape(b, l, RET_HEADS, RET_KEY_DIM), pos)
    rkh = _rope(rk.reshape(b, l, RET_HEADS, RET_KEY_DIM), pos) * (RET_KEY_DIM ** -0.5)
    rvh = rv.reshape(b, l, RET_HEADS, RET_VAL_DIM).astype(jnp.float32)
    ro, new_ret = _retention(rqh, rkh, rvh, ret_state)
    ro = ro * lax.rsqrt(jnp.mean(ro * ro, axis=-1, keepdims=True) + EPS)
    ro = (ro.reshape(b, l, RET_V_WIDTH) * jax.nn.silu(rg.astype(jnp.float32))).astype(xn.dtype)
    gates = jax.nn.sigmoid(g_logit.astype(jnp.float32)).reshape(b, l, N_BRANCH, D_MODEL)
    merged = (gates[:, :, 0] * jnp.matmul(fo, lw['w_br_fox']).astype(jnp.float32)
              + gates[:, :, 1] * jnp.matmul(po, lw['w_br_pool']).astype(jnp.float32)
              + gates[:, :, 2] * jnp.matmul(ro, lw['w_br_ret']).astype(jnp.float32))
    out = jnp.matmul(merged.astype(xn.dtype), lw['w_out'])
    return out, (fk, fv, logf, new_hist, new_ret)


def _memory_kv(mem, g, wk, wv):
    b, m, _ = mem.shape
    mn = _rmsnorm(mem, g)
    mk = jnp.matmul(mn, wk).reshape(b, m, MEM_HEADS, MEM_HEAD_DIM)
    mv = jnp.matmul(mn, wv).reshape(b, m, MEM_HEADS, MEM_HEAD_DIM)
    return mk, mv


def _cross_attention(xn, mk, mv, wq, wo):
    b, l, _ = xn.shape
    q = jnp.matmul(xn, wq).reshape(b, l, MEM_HEADS, MEM_HEAD_DIM)
    s = jnp.einsum('bqhd,bkhd->bhqk', q, mk.astype(q.dtype), preferred_element_type=jnp.float32) * (MEM_HEAD_DIM ** -0.5)
    p = jax.nn.softmax(s, axis=-1).astype(q.dtype)
    o = jnp.einsum('bhqk,bkhd->bqhd', p, mv.astype(q.dtype)).reshape(b, l, MEM_WIDTH)
    return jnp.matmul(o, wo)


def _swiglu(x, w1, w3, w2):
    return jnp.matmul(jax.nn.silu(jnp.matmul(x, w1)) * jnp.matmul(x, w3), w2)


def _moe_swiglu(x, router, w1, w3, w2):
    b, l, d = x.shape
    t = b * l
    xt = x.reshape(t, d)
    logits = jnp.matmul(xt, router).astype(jnp.float32)
    top_v, top_i = lax.top_k(logits, TOP_K)
    gate = jax.nn.softmax(top_v, axis=-1)
    flat_e = top_i.reshape(-1).astype(jnp.int32)
    flat_t = jnp.repeat(jnp.arange(t, dtype=jnp.int32), TOP_K)
    flat_g = gate.reshape(-1)
    order = jnp.argsort(flat_e)
    e_sorted = flat_e[order]
    counts = jnp.bincount(flat_e, length=N_EXPERTS).astype(jnp.int32)
    start = jnp.cumsum(counts) - counts
    padded = ((counts + MOE_BLOCK - 1) // MOE_BLOCK) * MOE_BLOCK
    padded_end = jnp.cumsum(padded)
    padded_start = padded_end - padded
    rank = jnp.arange(t * TOP_K, dtype=jnp.int32) - start[e_sorted]
    dest = padded_start[e_sorted] + rank
    n_blocks = -(-(t * TOP_K) // MOE_BLOCK) + N_EXPERTS
    n_slots = n_blocks * MOE_BLOCK
    slot_tok = jnp.full((n_slots,), t, jnp.int32).at[dest].set(flat_t[order])
    slot_gate = jnp.zeros((n_slots,), jnp.float32).at[dest].set(flat_g[order])
    block_start = jnp.arange(n_blocks, dtype=jnp.int32) * MOE_BLOCK
    block_exp = jnp.minimum(jnp.sum(block_start[:, None] >= padded_end[None, :], axis=1), N_EXPERTS - 1).astype(jnp.int32)
    xpad = jnp.concatenate([xt, jnp.zeros((1, d), xt.dtype)], axis=0)
    xb = xpad[slot_tok].reshape(n_blocks, MOE_BLOCK, d)

    def expert_block(args):
        xe, e = args
        return _swiglu(xe, w1[e], w3[e], w2[e])

    yb = lax.map(expert_block, (xb, block_exp)).reshape(n_slots, d)
    y = jnp.zeros((t + 1, d), jnp.float32).at[slot_tok].add(yb.astype(jnp.float32) * slot_gate[:, None])
    return y[:t].astype(x.dtype).reshape(b, l, d)


def _layer(x, lw, ffn, start_pos, fox_past, pool_hist, ret_state, mem_k, mem_v):
    mix, new_state = _token_mixing(_rmsnorm(x, lw['norm_mix_g']), lw, start_pos, fox_past, pool_hist, ret_state)
    x = x + mix
    x = x + _cross_attention(_rmsnorm(x, lw['norm_x_g']), mem_k, mem_v, lw['w_xq'], lw['w_xo'])
    x = x + ffn(_rmsnorm(x, lw['norm_ffn_g']))
    return x, new_state


def setup_inputs(seed: int = 0) -> dict:
    key = jax.random.key(seed)
    ks = jax.random.split(key, 34)
    D = D_MODEL

    def nrm(k, shape, scale=1.0):
        return jax.random.normal(k, shape, jnp.float32) * scale

    def gain(k, shape):
        return 1.0 + 0.02 * jax.random.normal(k, shape, jnp.float32)

    return {
        'x_prompt': nrm(ks[0], (BATCH, SEQ, D)),
        'x_sample': nrm(ks[1], (DEC_BATCH, DEC_SEQ, D)),
        'mem_prompt': nrm(ks[2], (BATCH, MEM_TOKENS, D)),
        'cache_fox_k': nrm(ks[3], (DEPTH, DEC_BATCH, PAST_LEN, FOX_HEADS, FOX_HEAD_DIM)),
        'cache_fox_v': nrm(ks[4], (DEPTH, DEC_BATCH, PAST_LEN, FOX_HEADS, FOX_HEAD_DIM)),
        'cache_fox_logf': jax.nn.log_sigmoid(nrm(ks[5], (DEPTH, DEC_BATCH, PAST_LEN, FOX_HEADS))),
        'state_pool': nrm(ks[6], (DEPTH, DEC_BATCH, POOL_HIST, POOL_WIDTH)),
        'state_ret': nrm(ks[7], (DEPTH, DEC_BATCH, RET_HEADS, RET_KEY_DIM, RET_VAL_DIM), 0.3),
        'cache_mem_k': nrm(ks[8], (DEPTH, DEC_BATCH, MEM_TOKENS, MEM_HEADS, MEM_HEAD_DIM)),
        'cache_mem_v': nrm(ks[9], (DEPTH, DEC_BATCH, MEM_TOKENS, MEM_HEADS, MEM_HEAD_DIM)),
        'norm_mix_g': gain(ks[10], (DEPTH, D)),
        'w_in': nrm(ks[11], (DEPTH, D, IN_WIDTH), D ** -0.5),
        'b_forget': nrm(ks[12], (DEPTH, FOX_HEADS), 0.1),
        'w_pool': nrm(ks[13], (DEPTH, POOL_GROUPS, POOL_GROUP_DIM, POOL_GROUP_DIM), POOL_GROUP_DIM ** -0.5),
        'pool_scale': gain(ks[14], (DEPTH, POOL_WIDTH)),
        'w_br_fox': nrm(ks[15], (DEPTH, FOX_WIDTH, D), FOX_WIDTH ** -0.5),
        'w_br_pool': nrm(ks[16], (DEPTH, POOL_WIDTH, D), POOL_WIDTH ** -0.5),
        'w_br_ret': nrm(ks[17], (DEPTH, RET_V_WIDTH, D), RET_V_WIDTH ** -0.5),
        'w_out': nrm(ks[18], (DEPTH, D, D), D ** -0.5),
        'norm_x_g': gain(ks[19], (DEPTH, D)),
        'norm_mem_g': gain(ks[20], (DEPTH, D)),
        'w_xq': nrm(ks[21], (DEPTH, D, MEM_WIDTH), D ** -0.5),
        'w_xk': nrm(ks[22], (DEPTH, D, MEM_WIDTH), D ** -0.5),
        'w_xv': nrm(ks[23], (DEPTH, D, MEM_WIDTH), D ** -0.5),
        'w_xo': nrm(ks[24], (DEPTH, MEM_WIDTH, D), MEM_WIDTH ** -0.5),
        'norm_ffn_g': gain(ks[25], (DEPTH, D)),
        'ffn_w1': nrm(ks[26], (N_DENSE, D, FFN_DIM), D ** -0.5),
        'ffn_w3': nrm(ks[27], (N_DENSE, D, FFN_DIM), D ** -0.5),
        'ffn_w2': nrm(ks[28], (N_DENSE, FFN_DIM, D), FFN_DIM ** -0.5),
        'moe_router': nrm(ks[29], (N_MOE, D, N_EXPERTS), D ** -0.5),
        'moe_w1': nrm(ks[30], (N_MOE, N_EXPERTS, D, EXPERT_DIM), D ** -0.5),
        'moe_w3': nrm(ks[31], (N_MOE, N_EXPERTS, D, EXPERT_DIM), D ** -0.5),
        'moe_w2': nrm(ks[32], (N_MOE, N_EXPERTS, EXPERT_DIM, D), EXPERT_DIM ** -0.5),
        'norm_final_g': gain(ks[33], (D,)),
    }


def reference(x_prompt, x_sample, mem_prompt, cache_fox_k, cache_fox_v, cache_fox_logf, state_pool, state_ret,
              cache_mem_k, cache_mem_v, norm_mix_g, w_in, b_forget, w_pool, pool_scale, w_br_fox, w_br_pool,
              w_br_ret, w_out, norm_x_g, norm_mem_g, w_xq, w_xk, w_xv, w_xo, norm_ffn_g, ffn_w1, ffn_w3, ffn_w2,
              moe_router, moe_w1, moe_w3, moe_w2, norm_final_g):
    yp = x_prompt
    ys = x_sample
    bp = x_prompt.shape[0]
    past_len = cache_fox_k.shape[2]
    pk_l, pv_l, plf_l, ppool_l, pret_l, pmk_l, pmv_l = [], [], [], [], [], [], []
    sk_l, sv_l, slf_l, spool_l, sret_l = [], [], [], [], []
    for l in range(DEPTH):
        lw = {'norm_mix_g': norm_mix_g[l], 'w_in': w_in[l], 'b_forget': b_forget[l], 'w_pool': w_pool[l],
              'pool_scale': pool_scale[l], 'w_br_fox': w_br_fox[l], 'w_br_pool': w_br_pool[l],
              'w_br_ret': w_br_ret[l], 'w_out': w_out[l], 'norm_x_g': norm_x_g[l], 'w_xq': w_xq[l],
              'w_xo': w_xo[l], 'norm_ffn_g': norm_ffn_g[l]}
        j = l // 2
        if l % 2 == 0:
            ffn = functools.partial(_swiglu, w1=ffn_w1[j], w3=ffn_w3[j], w2=ffn_w2[j])
        else:
            ffn = functools.partial(_moe_swiglu, router=moe_router[j], w1=moe_w1[j], w3=moe_w3[j], w2=moe_w2[j])
        mk, mv = _memory_kv(mem_prompt, norm_mem_g[l], w_xk[l], w_xv[l])
        hist0 = jnp.zeros((bp, POOL_HIST, POOL_WIDTH), yp.dtype)
        ret0 = jnp.zeros((bp, RET_HEADS, RET_KEY_DIM, RET_VAL_DIM), jnp.float32)
        yp, (fk, fv, lf, ph, rs) = _layer(yp, lw, ffn, 0, None, hist0, ret0, mk, mv)
        pk_l.append(fk); pv_l.append(fv); plf_l.append(lf); ppool_l.append(ph); pret_l.append(rs)
        pmk_l.append(mk); pmv_l.append(mv)
        ys, (sk, sv, slf, sph, srs) = _layer(ys, lw, ffn, past_len,
                                             (cache_fox_k[l], cache_fox_v[l], cache_fox_logf[l]),
                                             state_pool[l], state_ret[l], cache_mem_k[l], cache_mem_v[l])
        sk_l.append(sk); sv_l.append(sv); slf_l.append(slf); spool_l.append(sph); sret_l.append(srs)
    y_prompt = _rmsnorm(yp, norm_final_g)
    y_sample = _rmsnorm(ys, norm_final_g)
    p_fox_k = jnp.stack(pk_l)
    p_fox_v = jnp.stack(pv_l)
    p_fox_logf = jnp.stack(plf_l)
    p_pool = jnp.stack(ppool_l)
    p_ret = jnp.stack(pret_l)
    p_mem_k = jnp.stack(pmk_l)
    p_mem_v = jnp.stack(pmv_l)
    s_fox_k = jnp.stack(sk_l)
    s_fox_v = jnp.stack(sv_l)
    s_fox_logf = jnp.stack(slf_l)
    s_pool = jnp.stack(spool_l)
    s_ret = jnp.stack(sret_l)
    return (y_prompt, y_sample, p_fox_k, p_fox_v, p_fox_logf, p_pool, p_ret, p_mem_k, p_mem_v,
            s_fox_k, s_fox_v, s_fox_logf, s_pool, s_ret)
```

```python
import functools
import math

import jax
import jax.numpy as jnp
from jax import lax
from jax.experimental import pallas as pl
from jax.experimental.pallas import tpu as pltpu

EPS = 1e-6
NEG_INF = -1e30
POOL_WINDOWS = (2, 4, 8, 16)
ROPE_BASE = 10000.0
TOP_K = 2

LANE = 128
HALO = 16
V7X_VMEM_LIMIT = 56 * 1024 * 1024
MOE_BLOCK = 512

F32 = jnp.float32
BF16 = jnp.bfloat16


def _tile(n, pref, mult=8):
    if n <= pref:
        return n
    t = (pref // mult) * mult
    while t >= mult:
        if n % t == 0:
            return t
        t -= mult
    return n


def _params(*sem):
    return pltpu.CompilerParams(dimension_semantics=sem, vmem_limit_bytes=V7X_VMEM_LIMIT)


def _dot(a, b):
    return jnp.dot(a.astype(BF16), b.astype(BF16), preferred_element_type=F32)


def _dot_nt(a, b):
    return lax.dot_general(a.astype(BF16), b.astype(BF16), (((1,), (1,)), ((), ())), preferred_element_type=F32)


def _dot_tn(a, b):
    return lax.dot_general(a.astype(BF16), b.astype(BF16), (((0,), (0,)), ((), ())), preferred_element_type=F32)


def _sigmoid(x):
    return 1.0 / (1.0 + jnp.exp(-x))


def _rmsnorm_body(x_ref, g_ref, o_ref):
    x = x_ref[...]
    y = x * lax.rsqrt(jnp.mean(x * x, axis=-1, keepdims=True) + EPS)
    o_ref[...] = (y * g_ref[...]).astype(o_ref.dtype)


def _rmsnorm(x, g, out_dtype):
    t, d = x.shape
    tr = _tile(t, 256)
    return pl.pallas_call(
        _rmsnorm_body,
        grid=(t // tr,),
        in_specs=[pl.BlockSpec((tr, d), lambda i: (i, 0)), pl.BlockSpec((1, d), lambda i: (0, 0))],
        out_specs=pl.BlockSpec((tr, d), lambda i: (i, 0)),
        out_shape=jax.ShapeDtypeStruct((t, d), out_dtype),
        compiler_params=_params("parallel"),
        name="rmsnorm",
    )(x, g.reshape(1, d).astype(F32))


def _split3(x):
    h = x.astype(BF16)
    r = x - h.astype(F32)
    m = r.astype(BF16)
    l = (r - m.astype(F32)).astype(BF16)
    return h, m, l


def _rmsnorm_router_body(x_ref, g_ref, w_ref, o_ref, sel_ref, *, n_exp):
    x = x_ref[...]
    y = x * lax.rsqrt(jnp.mean(x * x, axis=-1, keepdims=True) + EPS) * g_ref[...]
    o_ref[...] = y.astype(o_ref.dtype)
    yh, ym, _ = _split3(y)
    w = w_ref[...]
    wh, wm, _ = _split3(w)
    logits = (jnp.dot(yh, wh, preferred_element_type=F32) + jnp.dot(yh, wm, preferred_element_type=F32)
              + jnp.dot(ym, wh, preferred_element_type=F32))
    lane = lax.broadcasted_iota(jnp.int32, logits.shape, 1)
    logits = jnp.where(lane < n_exp, logits, NEG_INF)
    m1 = jnp.max(logits, axis=-1, keepdims=True)
    i1 = jnp.min(jnp.where(logits == m1, lane, LANE), axis=-1, keepdims=True)
    rest = jnp.where(lane == i1, NEG_INF, logits)
    m2 = jnp.max(rest, axis=-1, keepdims=True)
    i2 = jnp.min(jnp.where(rest == m2, lane, LANE), axis=-1, keepdims=True)
    e2 = jnp.exp(m2 - m1)
    den = 1.0 + e2
    g1 = 1.0 / den
    g2 = e2 / den
    sel = jnp.where(lane == 0, i1.astype(F32),
                    jnp.where(lane == 1, i2.astype(F32),
                              jnp.where(lane == 2, g1, jnp.where(lane == 3, g2, 0.0))))
    sel_ref[...] = sel


def _rmsnorm_router(x, g, router):
    t, d = x.shape
    n_exp = router.shape[1]
    tr = _tile(t, 256)
    wpad = jnp.zeros((d, LANE), F32).at[:, :n_exp].set(router.astype(F32))
    return pl.pallas_call(
        functools.partial(_rmsnorm_router_body, n_exp=n_exp),
        grid=(t // tr,),
        in_specs=[pl.BlockSpec((tr, d), lambda i: (i, 0)), pl.BlockSpec((1, d), lambda i: (0, 0)),
                  pl.BlockSpec((d, LANE), lambda i: (0, 0))],
        out_specs=[pl.BlockSpec((tr, d), lambda i: (i, 0)), pl.BlockSpec((tr, LANE), lambda i: (i, 0))],
        out_shape=[jax.ShapeDtypeStruct((t, d), BF16), jax.ShapeDtypeStruct((t, LANE), F32)],
        compiler_params=_params("parallel"),
        name="rmsnorm_router",
    )(x, g.reshape(1, d).astype(F32), wpad)


def _mm_body(*refs, nk, has_res, grouped):
    if grouped:
        _, nused_ref, *refs = refs
    a_ref, b_ref, *refs = refs
    if has_res:
        r_ref, *refs = refs
    o_ref, *scr = refs

    def finish(acc):
        if has_res:
            acc = r_ref[...] + acc
        o_ref[...] = acc.astype(o_ref.dtype)

    def compute():
        part = _dot(a_ref[...], b_ref[...])
        if nk == 1:
            finish(part)
        else:
            k = pl.program_id(2)
            acc_ref = scr[0]

            @pl.when(k == 0)
            def _():
                acc_ref[...] = part

            @pl.when(k > 0)
            def _():
                acc_ref[...] += part

            @pl.when(k == nk - 1)
            def _():
                finish(acc_ref[...])

    if grouped:
        live = pl.program_id(0) < nused_ref[0]
        pl.when(live)(compute)

        @pl.when(jnp.logical_not(live))
        def _():
            o_ref[...] = jnp.zeros_like(o_ref)
    else:
        compute()


def _matmul(a, b, *, out_dtype, res=None, a_cols=None, b_cols=None, n=None, group=None,
            tm=1024, tn=1024, tk=4096, name="matmul"):
    m = a.shape[0]
    kdim = b.shape[-2]
    if n is None:
        n = b.shape[-1]
    a_cols = 0 if a_cols is None else a_cols
    b_cols = 0 if b_cols is None else b_cols
    tm = MOE_BLOCK if group is not None else _tile(m, tm)
    tn = _tile(math.gcd(n, b_cols) if b_cols else n, tn, LANE)
    tk = _tile(math.gcd(kdim, a_cols) if a_cols else kdim, tk, LANE)
    nk = kdim // tk
    ao, bo = a_cols // tk, b_cols // tn
    assert m % tm == 0 and n % tn == 0 and kdim % tk == 0 and a_cols % tk == 0 and b_cols % tn == 0
    grid = (m // tm, n // tn, nk)
    grouped = group is not None
    if grouped:
        a_map = lambda i, j, k, be, nu: (i, k + ao)
        b_spec = pl.BlockSpec((None, tk, tn), lambda i, j, k, be, nu: (be[i], k, j + bo))
        o_map = lambda i, j, k, be, nu: (i, j)
    else:
        a_map = lambda i, j, k: (i, k + ao)
        b_spec = pl.BlockSpec((tk, tn), lambda i, j, k: (k, j + bo))
        o_map = lambda i, j, k: (i, j)
    in_specs = [pl.BlockSpec((tm, tk), a_map), b_spec]
    args = [a, b]
    if res is not None:
        in_specs.append(pl.BlockSpec((tm, tn), o_map))
        args.append(res)
    scratch = [pltpu.VMEM((tm, tn), F32)] if nk > 1 else []
    body = functools.partial(_mm_body, nk=nk, has_res=res is not None, grouped=grouped)
    out_shape = jax.ShapeDtypeStruct((m, n), out_dtype)
    out_spec = pl.BlockSpec((tm, tn), o_map)
    cparams = _params("parallel", "parallel", "arbitrary")
    if grouped:
        gs = pltpu.PrefetchScalarGridSpec(num_scalar_prefetch=2, grid=grid, in_specs=in_specs,
                                          out_specs=out_spec, scratch_shapes=scratch)
        return pl.pallas_call(body, grid_spec=gs, out_shape=out_shape, compiler_params=cparams,
                              name=name)(group[0], group[1], *args)
    return pl.pallas_call(body, grid=grid, in_specs=in_specs, out_specs=out_spec, out_shape=out_shape,
                          scratch_shapes=scratch, compiler_params=cparams, name=name)(*args)


def _glu_body(*refs, grouped):
    if grouped:
        _, nused_ref, *refs = refs
    a_ref, w1_ref, w3_ref, o_ref = refs

    def compute():
        a = a_ref[...]
        h1 = _dot(a, w1_ref[...])
        h3 = _dot(a, w3_ref[...])
        o_ref[...] = (h1 * _sigmoid(h1) * h3).astype(o_ref.dtype)

    if grouped:
        live = pl.program_id(0) < nused_ref[0]
        pl.when(live)(compute)

        @pl.when(jnp.logical_not(live))
        def _():
            o_ref[...] = jnp.zeros_like(o_ref)
    else:
        compute()


def _glu(a, w1, w3, *, group=None, tm=1024, tn=512, name="glu"):
    m, kdim = a.shape
    n = w1.shape[-1]
    grouped = group is not None
    tm = MOE_BLOCK if grouped else _tile(m, tm)
    tn = _tile(n, tn, LANE)
    assert m % tm == 0 and n % tn == 0
    grid = (m // tm, n // tn)
    if grouped:
        in_specs = [pl.BlockSpec((tm, kdim), lambda i, j, be, nu: (i, 0)),
                    pl.BlockSpec((None, kdim, tn), lambda i, j, be, nu: (be[i], 0, j)),
                    pl.BlockSpec((None, kdim, tn), lambda i, j, be, nu: (be[i], 0, j))]
        out_spec = pl.BlockSpec((tm, tn), lambda i, j, be, nu: (i, j))
    else:
        in_specs = [pl.BlockSpec((tm, kdim), lambda i, j: (i, 0)),
                    pl.BlockSpec((kdim, tn), lambda i, j: (0, j)),
                    pl.BlockSpec((kdim, tn), lambda i, j: (0, j))]
        out_spec = pl.BlockSpec((tm, tn), lambda i, j: (i, j))
    body = functools.partial(_glu_body, grouped=grouped)
    out_shape = jax.ShapeDtypeStruct((m, n), BF16)
    cparams = _params("parallel", "arbitrary")
    if grouped:
        gs = pltpu.PrefetchScalarGridSpec(num_scalar_prefetch=2, grid=grid, in_specs=in_specs, out_specs=out_spec)
        return pl.pallas_call(body, grid_spec=gs, out_shape=out_shape, compiler_params=cparams,
                              name=name)(group[0], group[1], a, w1, w3)
    return pl.pallas_call(body, grid=grid, in_specs=in_specs, out_specs=out_spec, out_shape=out_shape,
                          compiler_params=cparams, name=name)(a, w1, w3)


def _logf_body(a_ref, w_ref, b_ref, o_ref):
    z = _dot(a_ref[...], w_ref[...]) + b_ref[...]
    o_ref[...] = jnp.minimum(z, 0.0) - jnp.log1p(jnp.exp(-jnp.abs(z)))


def _logf(xn, w_f, b_f):
    t, d = xn.shape
    nh = w_f.shape[1]
    wpad = jnp.zeros((d, LANE), BF16).at[:, :nh].set(w_f.astype(BF16))
    bpad = jnp.zeros((1, LANE), F32).at[0, :nh].set(b_f.astype(F32))
    tr = _tile(t, 1024)
    out = pl.pallas_call(
        _logf_body,
        grid=(t // tr,),
        in_specs=[pl.BlockSpec((tr, d), lambda i: (i, 0)), pl.BlockSpec((d, LANE), lambda i: (0, 0)),
                  pl.BlockSpec((1, LANE), lambda i: (0, 0))],
        out_specs=pl.BlockSpec((tr, LANE), lambda i: (i, 0)),
        out_shape=jax.ShapeDtypeStruct((t, LANE), F32),
        compiler_params=_params("parallel"),
        name="logf",
    )(xn, wpad, bpad)
    return out[:, :nh]


def _cumsum_body(x_ref, o_ref, *, nchunk):
    r = lax.broadcasted_iota(jnp.int32, (LANE, LANE), 0)
    c = lax.broadcasted_iota(jnp.int32, (LANE, LANE), 1)
    tri = (r <= c).astype(BF16)
    rows = x_ref.shape[1]

    def step(i, carry):
        h, m, l = _split3(x_ref[i])
        cs = (jnp.dot(h, tri, preferred_element_type=F32) + jnp.dot(m, tri, preferred_element_type=F32)
              + jnp.dot(l, tri, preferred_element_type=F32)) + carry
        o_ref[i] = cs
        return cs[:, LANE - 1:LANE]

    lax.fori_loop(0, nchunk, step, jnp.zeros((rows, 1), F32))


def _cumsum_time(x):
    b, l, nh = x.shape
    lp = -(-l // LANE) * LANE
    nchunk = lp // LANE
    rows = b * nh
    rp = -(-rows // 8) * 8
    xt = jnp.transpose(x, (0, 2, 1)).reshape(rows, l)
    xt = jnp.pad(xt, ((0, rp - rows), (0, lp - l)))
    xc = jnp.transpose(xt.reshape(rp, nchunk, LANE), (1, 0, 2))
    out = pl.pallas_call(
        functools.partial(_cumsum_body, nchunk=nchunk),
        out_shape=jax.ShapeDtypeStruct((nchunk, rp, LANE), F32),
        compiler_params=pltpu.CompilerParams(vmem_limit_bytes=V7X_VMEM_LIMIT),
        name="cumsum_time",
    )(xc)
    out = jnp.transpose(out, (1, 0, 2)).reshape(rp, lp)[:rows, :l]
    return out.reshape(b, nh, l)


def _fox_update(q, k, v, cq, ck, mask, m_ref, l_ref, acc_ref, hd, dh, scale):
    s = _dot_nt(q, k) * scale
    s = s + cq - ck
    if mask is not None:
        s = jnp.where(mask, s, NEG_INF)
    m_prev = m_ref[hd]
    m_new = jnp.maximum(m_prev, jnp.max(s, axis=-1, keepdims=True))
    alpha = jnp.exp(m_prev - m_new)
    p = jnp.exp(s - m_new)
    l_ref[hd] = alpha * l_ref[hd] + jnp.sum(p, axis=-1, keepdims=True)
    cols = slice(hd * dh, (hd + 1) * dh)
    acc_ref[:, cols] = alpha * acc_ref[:, cols] + _dot(p, v)
    m_ref[hd] = m_new


def _fox_init(m_ref, l_ref, acc_ref):
    m_ref[...] = jnp.full(m_ref.shape, NEG_INF, F32)
    l_ref[...] = jnp.zeros(l_ref.shape, F32)
    acc_ref[...] = jnp.zeros(acc_ref.shape, F32)


def _fox_finish(o_ref, m_ref, l_ref, acc_ref, nh, dh):
    for hd in range(nh):
        cols = slice(hd * dh, (hd + 1) * dh)
        o_ref[:, cols] = (acc_ref[:, cols] / l_ref[hd]).astype(o_ref.dtype)


def _fox_prompt_body(q_ref, k_ref, v_ref, cq_ref, ck_ref, o_ref, m_ref, l_ref, acc_ref, *, nh, dh, tq, nkv):
    qi = pl.program_id(1)
    kj = pl.program_id(2)

    @pl.when(kj == 0)
    def _():
        _fox_init(m_ref, l_ref, acc_ref)

    @pl.when(kj <= qi)
    def _():
        row = qi * tq + lax.broadcasted_iota(jnp.int32, (tq, tq), 0)
        col = kj * tq + lax.broadcasted_iota(jnp.int32, (tq, tq), 1)
        mask = col <= row
        for hd in range(nh):
            cols = slice(hd * dh, (hd + 1) * dh)
            _fox_update(q_ref[:, cols], k_ref[:, cols], v_ref[:, cols], cq_ref[hd], ck_ref[hd], mask,
                        m_ref, l_ref, acc_ref, hd, dh, dh ** -0.5)

    @pl.when(kj == nkv - 1)
    def _():
        _fox_finish(o_ref, m_ref, l_ref, acc_ref, nh, dh)


def _fox_prompt(h, csum, *, nb, seq, nh, dh, col_q, col_k, col_v):
    fw = nh * dh
    tq = _tile(seq, 512)
    nq = seq // tq
    assert col_q % fw == 0 and col_k % fw == 0 and col_v % fw == 0
    cq = csum.reshape(nb, nh, seq, 1)
    ck = csum.reshape(nb, nh, 1, seq)
    body = functools.partial(_fox_prompt_body, nh=nh, dh=dh, tq=tq, nkv=nq)
    return pl.pallas_call(
        body,
        grid=(nb, nq, nq),
        in_specs=[
            pl.BlockSpec((tq, fw), lambda b, i, j: (b * nq + i, col_q // fw)),
            pl.BlockSpec((tq, fw), lambda b, i, j: (b * nq + jnp.minimum(i, j), col_k // fw)),
            pl.BlockSpec((tq, fw), lambda b, i, j: (b * nq + jnp.minimum(i, j), col_v // fw)),
            pl.BlockSpec((None, nh, tq, 1), lambda b, i, j: (b, 0, i, 0)),
            pl.BlockSpec((None, nh, 1, tq), lambda b, i, j: (b, 0, 0, jnp.minimum(i, j))),
        ],
        out_specs=pl.BlockSpec((tq, fw), lambda b, i, j: (b * nq + i, 0)),
        out_shape=jax.ShapeDtypeStruct((nb * seq, fw), BF16),
        scratch_shapes=[pltpu.VMEM((nh, tq, 1), F32), pltpu.VMEM((nh, tq, 1), F32), pltpu.VMEM((tq, fw), F32)],
        compiler_params=_params("parallel", "parallel", "arbitrary"),
        name="fox_prompt",
    )(h, h, h, cq, ck)


def _fox_sample_body(q_ref, kn_ref, vn_ref, kp_ref, vp_ref, cq_ref, ckp_ref, ckn_ref, o_ref,
                     m_ref, l_ref, acc_ref, *, nh, dh, lq, npast):
    kj = pl.program_id(1)
    scale = dh ** -0.5

    @pl.when(kj == 0)
    def _():
        _fox_init(m_ref, l_ref, acc_ref)

    @pl.when(kj < npast)
    def _():
        for hd in range(nh):
            cols = slice(hd * dh, (hd + 1) * dh)
            _fox_update(q_ref[:, cols], kp_ref[:, cols], vp_ref[:, cols], cq_ref[hd], ckp_ref[hd], None,
                        m_ref, l_ref, acc_ref, hd, dh, scale)

    @pl.when(kj == npast)
    def _():
        row = lax.broadcasted_iota(jnp.int32, (lq, lq), 0)
        col = lax.broadcasted_iota(jnp.int32, (lq, lq), 1)
        mask = col <= row
        for hd in range(nh):
            cols = slice(hd * dh, (hd + 1) * dh)
            _fox_update(q_ref[:, cols], kn_ref[:, cols], vn_ref[:, cols], cq_ref[hd], ckn_ref[hd], mask,
                        m_ref, l_ref, acc_ref, hd, dh, scale)
        _fox_finish(o_ref, m_ref, l_ref, acc_ref, nh, dh)


def _fox_sample(h, past_k, past_v, csum, *, row0, nb, lq, nh, dh, col_q, col_k, col_v):
    fw = nh * dh
    plen = past_k.shape[1]
    tkp = _tile(plen, 1024)
    npast = plen // tkp
    assert row0 % lq == 0 and col_q % fw == 0 and col_k % fw == 0 and col_v % fw == 0
    rb = row0 // lq
    cq = csum[:, :, plen:].reshape(nb, nh, lq, 1)
    ckp = csum[:, :, :plen].reshape(nb, nh, 1, plen)
    ckn = csum[:, :, plen:].reshape(nb, nh, 1, lq)
    body = functools.partial(_fox_sample_body, nh=nh, dh=dh, lq=lq, npast=npast)
    pmap = lambda b, j: (b, jnp.minimum(j, npast - 1), 0)
    return pl.pallas_call(
        body,
        grid=(nb, npast + 1),
        in_specs=[
            pl.BlockSpec((lq, fw), lambda b, j: (rb + b, col_q // fw)),
            pl.BlockSpec((lq, fw), lambda b, j: (rb + b, col_k // fw)),
            pl.BlockSpec((lq, fw), lambda b, j: (rb + b, col_v // fw)),
            pl.BlockSpec((None, tkp, fw), pmap),
            pl.BlockSpec((None, tkp, fw), pmap),
            pl.BlockSpec((None, nh, lq, 1), lambda b, j: (b, 0, 0, 0)),
            pl.BlockSpec((None, nh, 1, tkp), lambda b, j: (b, 0, 0, jnp.minimum(j, npast - 1))),
            pl.BlockSpec((None, nh, 1, lq), lambda b, j: (b, 0, 0, 0)),
        ],
        out_specs=pl.BlockSpec((lq, fw), lambda b, j: (b, 0)),
        out_shape=jax.ShapeDtypeStruct((nb * lq, fw), BF16),
        scratch_shapes=[pltpu.VMEM((nh, lq, 1), F32), pltpu.VMEM((nh, lq, 1), F32), pltpu.VMEM((lq, fw), F32)],
        compiler_params=_params("parallel", "arbitrary"),
        name="fox_sample",
    )(h, h, h, past_k, past_v, cq, ckp, ckn)


def _pool_body(u_ref, prev_ref, hist_ref, w_ref, sc_ref, o_ref, ext_ref, *, tl, gd, start_pos):
    i = pl.program_id(1)
    first = i == 0
    ext_ref[0:HALO, :] = jnp.where(first, hist_ref[...], prev_ref[...])
    ext_ref[HALO:HALO + tl, :] = u_ref[...]
    pos = start_pos + i * tl + lax.broadcasted_iota(jnp.int32, (tl, 1), 0)
    for g, w in enumerate(POOL_WINDOWS):
        cols = slice(g * gd, (g + 1) * gd)
        cur = ext_ref[HALO:HALO + tl, cols]
        win = cur
        for d in range(1, w):
            win = win + ext_ref[HALO - d:HALO - d + tl, cols]
        cnt = jnp.minimum(pos + 1, w).astype(F32)
        pooled = win / cnt - cur
        y = _dot(pooled, w_ref[g]) * sc_ref[:, cols]
        o_ref[:, cols] = y.astype(o_ref.dtype)


def _pool(h, hist, w_pool, pool_scale, *, row0, nb, seq, col_u, start_pos):
    ng, gd, _ = w_pool.shape
    c = ng * gd
    nhist = hist.shape[1]
    assert nhist < HALO and max(POOL_WINDOWS) - 1 <= nhist and ng == len(POOL_WINDOWS)
    tl = _tile(seq, 512, HALO)
    nl = seq // tl
    assert col_u % c == 0 and row0 % tl == 0 and tl % HALO == 0 and seq % HALO == 0
    rb = row0 // tl
    hb = tl // HALO
    hist_p = jnp.concatenate([jnp.zeros((nb, HALO - nhist, c), F32), hist.astype(F32)], axis=1)
    body = functools.partial(_pool_body, tl=tl, gd=gd, start_pos=start_pos)
    return pl.pallas_call(
        body,
        grid=(nb, nl),
        in_specs=[
            pl.BlockSpec((tl, c), lambda b, i: (rb + b * nl + i, col_u // c)),
            pl.BlockSpec((HALO, c), lambda b, i: (jnp.maximum((rb + b * nl + i) * hb - 1, 0), col_u // c)),
            pl.BlockSpec((None, HALO, c), lambda b, i: (b, 0, 0)),
            pl.BlockSpec((ng, gd, gd), lambda b, i: (0, 0, 0)),
            pl.BlockSpec((1, c), lambda b, i: (0, 0)),
        ],
        out_specs=pl.BlockSpec((tl, c), lambda b, i: (b * nl + i, 0)),
        out_shape=jax.ShapeDtypeStruct((nb * seq, c), BF16),
        scratch_shapes=[pltpu.VMEM((HALO + tl, c), F32)],
        compiler_params=_params("parallel", "arbitrary"),
        name="pool",
    )(h, h, hist_p, w_pool.astype(BF16), pool_scale.reshape(1, c).astype(F32))


def _ret_body(q_ref, k_ref, v_ref, g_ref, cos_ref, sin_ref, ld_ref, s0_ref, o_ref, sout_ref, s_ref, *, ch, dk, nchunk):
    n = pl.program_id(2)

    @pl.when(n == 0)
    def _():
        s_ref[...] = s0_ref[...]

    ld = ld_ref[:, 0:1]
    cos_t = cos_ref[...]
    sin_t = sin_ref[...]

    def rope(x):
        return x * cos_t + pltpu.roll(x, dk // 2, axis=1) * sin_t

    q = rope(q_ref[...])
    k = rope(k_ref[...]) * (dk ** -0.5)
    v = v_ref[...]
    ii = lax.broadcasted_iota(jnp.int32, (ch, 1), 0).astype(F32)
    diff = ii - lax.broadcasted_iota(jnp.int32, (1, ch), 1).astype(F32)
    dmask = jnp.where(diff >= 0, jnp.exp(ld * jnp.maximum(diff, 0.0)), 0.0)
    scores = _dot_nt(q, k) * dmask
    o = _dot(scores, v)
    s_prev = s_ref[...]
    o = o + _dot(q * jnp.exp(ld * (ii + 1.0)), s_prev)
    k_dec = k * jnp.exp(ld * (ch - 1.0 - ii))
    s_ref[...] = jnp.exp(ld * ch) * s_prev + _dot_tn(k_dec, v)
    o = o * lax.rsqrt(jnp.mean(o * o, axis=-1, keepdims=True) + EPS)
    gate = g_ref[...]
    o_ref[...] = (o * (gate * _sigmoid(gate))).astype(o_ref.dtype)

    @pl.when(n == nchunk - 1)
    def _():
        sout_ref[...] = s_ref[...]


def _retention(h, state0, *, row0, nb, seq, col_q, col_k, col_v, col_g, start_pos):
    _, nh, dk, dv = state0.shape
    ch = _tile(seq, 256)
    nchunk = seq // ch
    assert row0 % ch == 0 and col_q % dk == 0 and col_k % dk == 0 and col_v % dv == 0 and col_g % dv == 0
    rb = row0 // ch
    half = dk // 2
    inv = ROPE_BASE ** (-jnp.arange(half, dtype=F32) / half)
    ang = (start_pos + jnp.arange(seq, dtype=jnp.int32)).astype(F32)[:, None] * inv[None, :]
    cos_t = jnp.concatenate([jnp.cos(ang), jnp.cos(ang)], axis=1)
    sin_t = jnp.concatenate([-jnp.sin(ang), jnp.sin(ang)], axis=1)
    ld = jnp.log1p(-jnp.exp2(-5.0 - jnp.arange(nh, dtype=F32)))
    ldv = jnp.broadcast_to(ld[:, None, None], (nh, 1, LANE))
    body = functools.partial(_ret_body, ch=ch, dk=dk, nchunk=nchunk)
    rows = lambda b, hd, n: rb + b * nchunk + n
    return pl.pallas_call(
        body,
        grid=(nb, nh, nchunk),
        in_specs=[
            pl.BlockSpec((ch, dk), lambda b, hd, n: (rows(b, hd, n), col_q // dk + hd)),
            pl.BlockSpec((ch, dk), lambda b, hd, n: (rows(b, hd, n), col_k // dk + hd)),
            pl.BlockSpec((ch, dv), lambda b, hd, n: (rows(b, hd, n), col_v // dv + hd)),
            pl.BlockSpec((ch, dv), lambda b, hd, n: (rows(b, hd, n), col_g // dv + hd)),
            pl.BlockSpec((ch, dk), lambda b, hd, n: (n, 0)),
            pl.BlockSpec((ch, dk), lambda b, hd, n: (n, 0)),
            pl.BlockSpec((None, 1, LANE), lambda b, hd, n: (hd, 0, 0)),
            pl.BlockSpec((None, None, dk, dv), lambda b, hd, n: (b, hd, 0, 0)),
        ],
        out_specs=[
            pl.BlockSpec((ch, dv), lambda b, hd, n: (b * nchunk + n, hd)),
            pl.BlockSpec((None, None, dk, dv), lambda b, hd, n: (b, hd, 0, 0)),
        ],
        out_shape=[jax.ShapeDtypeStruct((nb * seq, nh * dv), BF16),
                   jax.ShapeDtypeStruct((nb, nh, dk, dv), F32)],
        scratch_shapes=[pltpu.VMEM((dk, dv), F32)],
        compiler_params=_params("parallel", "parallel", "arbitrary"),
        name="retention",
    )(h, h, h, h, cos_t, sin_t, ldv, state0.astype(F32))


def _merge_body(fo_ref, po_ref, ro_ref, wf_ref, wp_ref, wr_ref, g0_ref, g1_ref, g2_ref, o_ref):
    acc = _sigmoid(g0_ref[...]) * _dot(fo_ref[...], wf_ref[...])
    acc = acc + _sigmoid(g1_ref[...]) * _dot(po_ref[...], wp_ref[...])
    acc = acc + _sigmoid(g2_ref[...]) * _dot(ro_ref[...], wr_ref[...])
    o_ref[...] = acc.astype(o_ref.dtype)


def _merge(fo, po, ro, wf, wp, wr, h, *, col_g, d):
    t = fo.shape[0]
    tm = _tile(t, 1024)
    tn = _tile(math.gcd(d, col_g), 512, LANE)
    gb = col_g // tn
    nd = d // tn
    kf, kp, kr = fo.shape[1], po.shape[1], ro.shape[1]
    return pl.pallas_call(
        _merge_body,
        grid=(t // tm, nd),
        in_specs=[
            pl.BlockSpec((tm, kf), lambda i, j: (i, 0)),
            pl.BlockSpec((tm, kp), lambda i, j: (i, 0)),
            pl.BlockSpec((tm, kr), lambda i, j: (i, 0)),
            pl.BlockSpec((kf, tn), lambda i, j: (0, j)),
            pl.BlockSpec((kp, tn), lambda i, j: (0, j)),
            pl.BlockSpec((kr, tn), lambda i, j: (0, j)),
            pl.BlockSpec((tm, tn), lambda i, j: (i, gb + j)),
            pl.BlockSpec((tm, tn), lambda i, j: (i, gb + nd + j)),
            pl.BlockSpec((tm, tn), lambda i, j: (i, gb + 2 * nd + j)),
        ],
        out_specs=pl.BlockSpec((tm, tn), lambda i, j: (i, j)),
        out_shape=jax.ShapeDtypeStruct((t, d), BF16),
        compiler_params=_params("parallel", "arbitrary"),
        name="merge",
    )(fo, po, ro, wf, wp, wr, h, h, h)


def _xattn_body(q_ref, k_ref, v_ref, o_ref, *, nh, dh):
    scale = dh ** -0.5
    for hd in range(nh):
        cols = slice(hd * dh, (hd + 1) * dh)
        s = _dot_nt(q_ref[:, cols], k_ref[:, cols]) * scale
        p = jnp.exp(s - jnp.max(s, axis=-1, keepdims=True))
        o = _dot(p, v_ref[:, cols]) / jnp.sum(p, axis=-1, keepdims=True)
        o_ref[:, cols] = o.astype(o_ref.dtype)


def _xattn(q, mk, mv, *, row0, nb, seq, nh, dh):
    w = nh * dh
    mt = mk.shape[1]
    tq = _tile(seq, 512)
    nq = seq // tq
    assert row0 % tq == 0
    rb = row0 // tq
    return pl.pallas_call(
        functools.partial(_xattn_body, nh=nh, dh=dh),
        grid=(nb, nq),
        in_specs=[
            pl.BlockSpec((tq, w), lambda b, i: (rb + b * nq + i, 0)),
            pl.BlockSpec((None, mt, w), lambda b, i: (b, 0, 0)),
            pl.BlockSpec((None, mt, w), lambda b, i: (b, 0, 0)),
        ],
        out_specs=pl.BlockSpec((tq, w), lambda b, i: (b * nq + i, 0)),
        out_shape=jax.ShapeDtypeStruct((nb * seq, w), BF16),
        compiler_params=_params("parallel", "arbitrary"),
        name="xattn",
    )(q, mk, mv)


def _rank_body(sel_ref, rk_ref, cnt_ref, carry_ref, *, tb):
    i = pl.program_id(0)

    @pl.when(i == 0)
    def _():
        carry_ref[...] = jnp.zeros_like(carry_ref)

    sel = sel_ref[...]
    lane = lax.broadcasted_iota(jnp.int32, sel.shape, 1)
    lanef = lane.astype(F32)
    e1 = sel[:, 0:1]
    e2 = sel[:, 1:2]
    onehot = jnp.logical_or(lanef == e1, lanef == e2).astype(F32)
    r = lax.broadcasted_iota(jnp.int32, (tb, tb), 0)
    c = lax.broadcasted_iota(jnp.int32, (tb, tb), 1)
    before = (c < r).astype(BF16)
    rank = jnp.dot(before, onehot.astype(BF16), preferred_element_type=F32) + carry_ref[...]
    r1 = jnp.sum(jnp.where(lanef == e1, rank, 0.0), axis=-1, keepdims=True)
    r2 = jnp.sum(jnp.where(lanef == e2, rank, 0.0), axis=-1, keepdims=True)
    rk_ref[...] = jnp.where(lane == 0, r1, jnp.where(lane == 1, r2, 0.0))
    total = carry_ref[...] + jnp.sum(onehot, axis=0, keepdims=True)
    carry_ref[...] = total
    cnt_ref[...] = total


def _expert_ranks(sel):
    t = sel.shape[0]
    tb = _tile(t, 512)
    return pl.pallas_call(
        functools.partial(_rank_body, tb=tb),
        grid=(t // tb,),
        in_specs=[pl.BlockSpec((tb, LANE), lambda i: (i, 0))],
        out_specs=[pl.BlockSpec((tb, LANE), lambda i: (i, 0)), pl.BlockSpec((1, LANE), lambda i: (0, 0))],
        out_shape=[jax.ShapeDtypeStruct((t, LANE), F32), jax.ShapeDtypeStruct((1, LANE), F32)],
        scratch_shapes=[pltpu.VMEM((1, LANE), F32)],
        compiler_params=_params("arbitrary"),
        name="expert_ranks",
    )(sel)


def _row_copy(src, dst, sem, s, d):
    return pltpu.make_async_copy(src.at[s], dst.at[d], sem)


def _dispatch_body(dest_ref, x_ref, zero_ref, xs_ref, sem, *, tb):
    del zero_ref
    base = pl.program_id(0) * tb

    def issue(r, c):
        for k in range(TOP_K):
            _row_copy(x_ref, xs_ref, sem, base + r, dest_ref[0, 0, TOP_K * r + k]).start()
        return c

    lax.fori_loop(0, tb, issue, 0)

    def drain(r, c):
        _row_copy(x_ref, xs_ref, sem, 0, 0).wait()
        return c

    lax.fori_loop(0, TOP_K * tb, drain, 0)


def _dispatch(x, dest, n_slots):
    t, d = x.shape
    tb = _tile(t, 256)
    s = d // LANE
    dest3 = dest.reshape(t // tb, 1, TOP_K * tb)
    zeros = jnp.zeros((n_slots, s, LANE), x.dtype)
    xs = pl.pallas_call(
        functools.partial(_dispatch_body, tb=tb),
        grid=(t // tb,),
        in_specs=[pl.BlockSpec((1, 1, TOP_K * tb), lambda i: (i, 0, 0), memory_space=pltpu.SMEM),
                  pl.BlockSpec(memory_space=pl.ANY), pl.BlockSpec(memory_space=pl.ANY)],
        out_specs=pl.BlockSpec(memory_space=pl.ANY),
        out_shape=jax.ShapeDtypeStruct((n_slots, s, LANE), x.dtype),
        scratch_shapes=[pltpu.SemaphoreType.DMA(())],
        input_output_aliases={2: 0},
        compiler_params=_params("arbitrary"),
        name="moe_dispatch",
    )(dest3, x.reshape(t, s, LANE), zeros)
    return xs.reshape(n_slots, d)


def _combine_body(dest_ref, g0_ref, g1_ref, x_ref, y_ref, o_ref, r0_ref, r1_ref, sem, *, tb):
    bufs = (r0_ref, r1_ref)

    def issue(r, c):
        for k in range(TOP_K):
            _row_copy(y_ref, bufs[k], sem, dest_ref[0, 0, TOP_K * r + k], r).start()
        return c

    lax.fori_loop(0, tb, issue, 0)

    def drain(r, c):
        _row_copy(y_ref, r0_ref, sem, 0, 0).wait()
        return c

    lax.fori_loop(0, TOP_K * tb, drain, 0)
    o_ref[...] = x_ref[...] + (r0_ref[...] * g0_ref[...] + r1_ref[...] * g1_ref[...])


def _combine(x, yb, dest, sel):
    t, d = x.shape
    tb = _tile(t, 256)
    s = d // LANE
    dest3 = dest.reshape(t // tb, 1, TOP_K * tb)
    g0 = sel[:, 2].reshape(t, 1, 1)
    g1 = sel[:, 3].reshape(t, 1, 1)
    row = pl.BlockSpec((tb, s, LANE), lambda i: (i, 0, 0))
    gate = pl.BlockSpec((tb, 1, 1), lambda i: (i, 0, 0))
    out = pl.pallas_call(
        functools.partial(_combine_body, tb=tb),
        grid=(t // tb,),
        in_specs=[pl.BlockSpec((1, 1, TOP_K * tb), lambda i: (i, 0, 0), memory_space=pltpu.SMEM),
                  gate, gate, row, pl.BlockSpec(memory_space=pl.ANY)],
        out_specs=row,
        out_shape=jax.ShapeDtypeStruct((t, s, LANE), F32),
        scratch_shapes=[pltpu.VMEM((tb, s, LANE), F32), pltpu.VMEM((tb, s, LANE), F32),
                        pltpu.SemaphoreType.DMA(())],
        compiler_params=_params("arbitrary"),
        name="moe_combine",
    )(dest3, g0, g1, x.reshape(t, s, LANE), yb.reshape(-1, s, LANE))
    return out.reshape(t, d)


def _moe_ffn(x, g, router, w1, w3, w2):
    t, d = x.shape
    n_exp = router.shape[1]
    xn, sel = _rmsnorm_router(x, g, router)
    rk, cnt = _expert_ranks(sel)
    counts = cnt[0, :n_exp].astype(jnp.int32)
    padded = ((counts + MOE_BLOCK - 1) // MOE_BLOCK) * MOE_BLOCK
    padded_end = jnp.cumsum(padded)
    padded_start = padded_end - padded
    n_blocks = -(-(t * TOP_K) // MOE_BLOCK) + n_exp
    n_slots = n_blocks * MOE_BLOCK
    eid = sel[:, :TOP_K].astype(jnp.int32)
    dest = (padded_start[eid] + rk[:, :TOP_K].astype(jnp.int32)).astype(jnp.int32)
    block_start = jnp.arange(n_blocks, dtype=jnp.int32) * MOE_BLOCK
    block_exp = jnp.minimum(jnp.sum(block_start[:, None] >= padded_end[None, :], axis=1), n_exp - 1).astype(jnp.int32)
    n_used = (padded_end[-1:] // MOE_BLOCK).astype(jnp.int32)
    group = (block_exp, n_used)
    xs = _dispatch(xn, dest, n_slots)
    hmid = _glu(xs, w1, w3, group=group, name="expert_glu")
    yb = _matmul(hmid, w2, out_dtype=F32, group=group, tk=2048, name="expert_down")
    return _combine(x, yb, dest, sel)


def _pad_cols(w, mult):
    n = w.shape[-1]
    return jnp.pad(w, [(0, 0)] * (w.ndim - 1) + [(0, -(-n // mult) * mult - n)])


def _pad_rows(w, mult):
    n = w.shape[-2]
    return jnp.pad(w, [(0, 0)] * (w.ndim - 2) + [(0, -(-n // mult) * mult - n), (0, 0)])


def kernel(x_prompt, x_sample, mem_prompt, cache_fox_k, cache_fox_v, cache_fox_logf, state_pool, state_ret, cache_mem_k, cache_mem_v, norm_mix_g, w_in, b_forget, w_pool, pool_scale, w_br_fox, w_br_pool, w_br_ret, w_out, norm_x_g, norm_mem_g, w_xq, w_xk, w_xv, w_xo, norm_ffn_g, ffn_w1, ffn_w3, ffn_w2, moe_router, moe_w1, moe_w3, moe_w2, norm_final_g):
    bp, sp, d = x_prompt.shape
    bs, ss, _ = x_sample.shape
    depth = w_in.shape[0]
    past = cache_fox_k.shape[2]
    nh_f, dh_f = cache_fox_k.shape[3], cache_fox_k.shape[4]
    fw = nh_f * dh_f
    nhist, pw = state_pool.shape[2], state_pool.shape[3]
    nh_r, dk_r, dv_r = state_ret.shape[2], state_ret.shape[3], state_ret.shape[4]
    rqk, rvw = nh_r * dk_r, nh_r * dv_r
    mt, nh_m, dh_m = cache_mem_k.shape[2], cache_mem_k.shape[3], cache_mem_k.shape[4]
    mw = nh_m * dh_m
    tp, ts = bp * sp, bs * ss
    assert sp >= nhist and ss >= nhist

    c_fq, c_fk, c_fv = 0, fw, 2 * fw
    c_pu = 3 * fw
    c_rq = c_pu + pw
    c_rk = c_rq + rqk
    c_rv = c_rk + rqk
    c_rg = c_rv + rvw
    c_g = c_rg + rvw
    assert w_in.shape[2] == c_g + 3 * d + nh_f

    x = jnp.concatenate([x_prompt.reshape(tp, d), x_sample.reshape(ts, d)], axis=0).astype(F32)
    outs = {k: [] for k in ("pk", "pv", "plf", "pp", "pr", "pmk", "pmv", "sk", "sv", "slf", "spool", "sr")}

    for l in range(depth):
        w_main = jnp.concatenate([w_in[l][:, :3 * fw], w_in[l][:, 3 * fw + nh_f:]], axis=1).astype(BF16)
        w_flog = w_in[l][:, 3 * fw:3 * fw + nh_f]

        xn = _rmsnorm(x, norm_mix_g[l], BF16)
        h = _matmul(xn, w_main, out_dtype=F32, name="proj_in")
        logf = _logf(xn, w_flog, b_forget[l])
        logf_p = logf[:tp].reshape(bp, sp, nh_f)
        logf_s = logf[tp:].reshape(bs, ss, nh_f)
        cs_p = _cumsum_time(logf_p)
        cs_s = _cumsum_time(jnp.concatenate([cache_fox_logf[l].astype(F32), logf_s], axis=1))
        fo_p = _fox_prompt(h, cs_p, nb=bp, seq=sp, nh=nh_f, dh=dh_f, col_q=c_fq, col_k=c_fk, col_v=c_fv)
        fo_s = _fox_sample(h, cache_fox_k[l].reshape(bs, past, fw), cache_fox_v[l].reshape(bs, past, fw), cs_s,
                           row0=tp, nb=bs, lq=ss, nh=nh_f, dh=dh_f, col_q=c_fq, col_k=c_fk, col_v=c_fv)
        fo = jnp.concatenate([fo_p, fo_s], axis=0)

        po_p = _pool(h, jnp.zeros((bp, nhist, pw), F32), w_pool[l], pool_scale[l],
                     row0=0, nb=bp, seq=sp, col_u=c_pu, start_pos=0)
        po_s = _pool(h, state_pool[l], w_pool[l], pool_scale[l],
                     row0=tp, nb=bs, seq=ss, col_u=c_pu, start_pos=past)
        po = jnp.concatenate([po_p, po_s], axis=0)

        ro_p, rs_p = _retention(h, jnp.zeros((bp, nh_r, dk_r, dv_r), F32), row0=0, nb=bp, seq=sp,
                                col_q=c_rq, col_k=c_rk, col_v=c_rv, col_g=c_rg, start_pos=0)
        ro_s, rs_s = _retention(h, state_ret[l], row0=tp, nb=bs, seq=ss,
                                col_q=c_rq, col_k=c_rk, col_v=c_rv, col_g=c_rg, start_pos=past)
        ro = jnp.concatenate([ro_p, ro_s], axis=0)

        merged = _merge(fo, po, ro, w_br_fox[l].astype(BF16), w_br_pool[l].astype(BF16),
                        w_br_ret[l].astype(BF16), h, col_g=c_g, d=d)
        x = _matmul(merged, w_out[l].astype(BF16), out_dtype=F32, res=x, name="proj_out")

        mn = _rmsnorm(mem_prompt.reshape(bp * mt, d).astype(F32), norm_mem_g[l], BF16)
        mk = _matmul(mn, w_xk[l].astype(BF16), out_dtype=F32, name="mem_k")
        mv = _matmul(mn, w_xv[l].astype(BF16), out_dtype=F32, name="mem_v")
        xn = _rmsnorm(x, norm_x_g[l], BF16)
        q = _matmul(xn, w_xq[l].astype(BF16), out_dtype=BF16, name="xattn_q")
        xo_p = _xattn(q, mk.reshape(bp, mt, mw), mv.reshape(bp, mt, mw), row0=0, nb=bp, seq=sp, nh=nh_m, dh=dh_m)
        xo_s = _xattn(q, cache_mem_k[l].reshape(bs, mt, mw).astype(F32), cache_mem_v[l].reshape(bs, mt, mw).astype(F32),
                      row0=tp, nb=bs, seq=ss, nh=nh_m, dh=dh_m)
        x = _matmul(jnp.concatenate([xo_p, xo_s], axis=0), w_xo[l].astype(BF16), out_dtype=F32, res=x,
                    name="xattn_out")

        j = l // 2
        if l % 2 == 0:
            xn = _rmsnorm(x, norm_ffn_g[l], BF16)
            w1 = _pad_cols(ffn_w1[j], 1024).astype(BF16)
            w3 = _pad_cols(ffn_w3[j], 1024).astype(BF16)
            w2 = _pad_rows(ffn_w2[j], 1024).astype(BF16)
            hmid = _glu(xn, w1, w3, name="ffn_glu")
            x = _matmul(hmid, w2, out_dtype=F32, res=x, tk=1024, name="ffn_down")
        else:
            x = _moe_ffn(x, norm_ffn_g[l], moe_router[j], moe_w1[j].astype(BF16), moe_w3[j].astype(BF16),
                         moe_w2[j].astype(BF16))

        outs["pk"].append(h[:tp, c_fk:c_fk + fw].reshape(bp, sp, nh_f, dh_f))
        outs["pv"].append(h[:tp, c_fv:c_fv + fw].reshape(bp, sp, nh_f, dh_f))
        outs["plf"].append(logf_p)
        outs["pp"].append(h[:tp, c_pu:c_pu + pw].reshape(bp, sp, pw)[:, sp - nhist:])
        outs["pr"].append(rs_p)
        outs["pmk"].append(mk.reshape(bp, mt, nh_m, dh_m))
        outs["pmv"].append(mv.reshape(bp, mt, nh_m, dh_m))
        outs["sk"].append(h[tp:, c_fk:c_fk + fw].reshape(bs, ss, nh_f, dh_f))
        outs["sv"].append(h[tp:, c_fv:c_fv + fw].reshape(bs, ss, nh_f, dh_f))
        outs["slf"].append(logf_s)
        outs["spool"].append(h[tp:, c_pu:c_pu + pw].reshape(bs, ss, pw)[:, ss - nhist:])
        outs["sr"].append(rs_s)

    y = _rmsnorm(x, norm_final_g, F32)
    st = {k: jnp.stack(v) for k, v in outs.items()}
    return (y[:tp].reshape(bp, sp, d), y[tp:].reshape(bs, ss, d),
            st["pk"], st["pv"], st["plf"], st["pp"], st["pr"], st["pmk"], st["pmv"],
            st["sk"], st["sv"], st["slf"], st["spool"], st["sr"])
```

```python
import functools
import math

import jax
import jax.numpy as jnp
from jax import lax
from jax.experimental import pallas as pl
from jax.experimental.pallas import tpu as pltpu

EPS = 1e-6
NEG_INF = -1e30
POOL_WINDOWS = (2, 4, 8, 16)
ROPE_BASE = 10000.0
TOP_K = 2

LANE = 128
HALO = 16
V7X_VMEM_LIMIT = 56 * 1024 * 1024
MOE_BLOCK = 1024
MOE_SUB = 256
MOE_NSUB = MOE_BLOCK // MOE_SUB

F32 = jnp.float32
BF16 = jnp.bfloat16


def _tile(n, pref, mult=8):
    if n <= pref:
        return n
    t = (pref // mult) * mult
    while t >= mult:
        if n % t == 0:
            return t
        t -= mult
    return n


def _params(*sem):
    return pltpu.CompilerParams(dimension_semantics=sem, vmem_limit_bytes=V7X_VMEM_LIMIT)


def _dot(a, b):
    return jnp.dot(a.astype(BF16), b.astype(BF16), preferred_element_type=F32)


def _dot_nt(a, b):
    return lax.dot_general(a.astype(BF16), b.astype(BF16), (((1,), (1,)), ((), ())), preferred_element_type=F32)


def _dot_tn(a, b):
    return lax.dot_general(a.astype(BF16), b.astype(BF16), (((0,), (0,)), ((), ())), preferred_element_type=F32)


def _sigmoid(x):
    return 1.0 / (1.0 + jnp.exp(-x))


def _rmsnorm_body(x_ref, g_ref, o_ref):
    x = x_ref[...]
    y = x * lax.rsqrt(jnp.mean(x * x, axis=-1, keepdims=True) + EPS)
    o_ref[...] = (y * g_ref[...]).astype(o_ref.dtype)


def _rmsnorm(x, g, out_dtype):
    t, d = x.shape
    tr = _tile(t, 256)
    return pl.pallas_call(
        _rmsnorm_body,
        grid=(t // tr,),
        in_specs=[pl.BlockSpec((tr, d), lambda i: (i, 0)), pl.BlockSpec((1, d), lambda i: (0, 0))],
        out_specs=pl.BlockSpec((tr, d), lambda i: (i, 0)),
        out_shape=jax.ShapeDtypeStruct((t, d), out_dtype),
        compiler_params=_params("parallel"),
        name="rmsnorm",
    )(x, g.reshape(1, d).astype(F32))


def _split3(x):
    h = x.astype(BF16)
    r = x - h.astype(F32)
    m = r.astype(BF16)
    l = (r - m.astype(F32)).astype(BF16)
    return h, m, l


def _rmsnorm_router_body(x_ref, g_ref, w_ref, o_ref, sel_ref, *, n_exp):
    x = x_ref[...]
    y = x * lax.rsqrt(jnp.mean(x * x, axis=-1, keepdims=True) + EPS) * g_ref[...]
    o_ref[...] = y.astype(o_ref.dtype)
    yh, ym, _ = _split3(y)
    w = w_ref[...]
    wh, wm, _ = _split3(w)
    logits = (jnp.dot(yh, wh, preferred_element_type=F32) + jnp.dot(yh, wm, preferred_element_type=F32)
              + jnp.dot(ym, wh, preferred_element_type=F32))
    lane = lax.broadcasted_iota(jnp.int32, logits.shape, 1)
    logits = jnp.where(lane < n_exp, logits, NEG_INF)
    m1 = jnp.max(logits, axis=-1, keepdims=True)
    i1 = jnp.min(jnp.where(logits == m1, lane, LANE), axis=-1, keepdims=True)
    rest = jnp.where(lane == i1, NEG_INF, logits)
    m2 = jnp.max(rest, axis=-1, keepdims=True)
    i2 = jnp.min(jnp.where(rest == m2, lane, LANE), axis=-1, keepdims=True)
    e2 = jnp.exp(m2 - m1)
    den = 1.0 + e2
    g1 = 1.0 / den
    g2 = e2 / den
    sel = jnp.where(lane == 0, i1.astype(F32),
                    jnp.where(lane == 1, i2.astype(F32),
                              jnp.where(lane == 2, g1, jnp.where(lane == 3, g2, 0.0))))
    sel_ref[...] = sel


def _rmsnorm_router(x, g, router):
    t, d = x.shape
    n_exp = router.shape[1]
    tr = _tile(t, 256)
    wpad = jnp.zeros((d, LANE), F32).at[:, :n_exp].set(router.astype(F32))
    return pl.pallas_call(
        functools.partial(_rmsnorm_router_body, n_exp=n_exp),
        grid=(t // tr,),
        in_specs=[pl.BlockSpec((tr, d), lambda i: (i, 0)), pl.BlockSpec((1, d), lambda i: (0, 0)),
                  pl.BlockSpec((d, LANE), lambda i: (0, 0))],
        out_specs=[pl.BlockSpec((tr, d), lambda i: (i, 0)), pl.BlockSpec((tr, LANE), lambda i: (i, 0))],
        out_shape=[jax.ShapeDtypeStruct((t, d), BF16), jax.ShapeDtypeStruct((t, LANE), F32)],
        compiler_params=_params("parallel"),
        name="rmsnorm_router",
    )(x, g.reshape(1, d).astype(F32), wpad)


def _mm_body(a_ref, b_ref, *refs, nk, has_res):
    if has_res:
        r_ref, *refs = refs
    o_ref, *scr = refs

    def finish(acc):
        if has_res:
            acc = r_ref[...] + acc
        o_ref[...] = acc.astype(o_ref.dtype)

    part = _dot(a_ref[...], b_ref[...])
    if nk == 1:
        finish(part)
    else:
        k = pl.program_id(2)
        acc_ref = scr[0]

        @pl.when(k == 0)
        def _():
            acc_ref[...] = part

        @pl.when(k > 0)
        def _():
            acc_ref[...] += part

        @pl.when(k == nk - 1)
        def _():
            finish(acc_ref[...])


def _matmul(a, b, *, out_dtype, res=None, tm=1024, tn=1024, tk=4096, name="matmul"):
    m, kdim = a.shape
    n = b.shape[-1]
    tm = _tile(m, tm)
    tn = _tile(n, tn, LANE)
    tk = _tile(kdim, tk, LANE)
    nk = kdim // tk
    assert m % tm == 0 and n % tn == 0 and kdim % tk == 0
    in_specs = [pl.BlockSpec((tm, tk), lambda i, j, k: (i, k)), pl.BlockSpec((tk, tn), lambda i, j, k: (k, j))]
    args = [a, b]
    if res is not None:
        in_specs.append(pl.BlockSpec((tm, tn), lambda i, j, k: (i, j)))
        args.append(res)
    return pl.pallas_call(
        functools.partial(_mm_body, nk=nk, has_res=res is not None),
        grid=(m // tm, n // tn, nk),
        in_specs=in_specs,
        out_specs=pl.BlockSpec((tm, tn), lambda i, j, k: (i, j)),
        out_shape=jax.ShapeDtypeStruct((m, n), out_dtype),
        scratch_shapes=[pltpu.VMEM((tm, tn), F32)] if nk > 1 else [],
        compiler_params=_params("parallel", "parallel", "arbitrary"),
        name=name,
    )(*args)


def _mm_ws_body(a_ref, w_ref, *refs, has_res):
    if has_res:
        r_ref, *refs = refs
    o_ref, wb_ref = refs

    @pl.when(pl.program_id(1) == 0)
    def _():
        wb_ref[...] = w_ref[...].astype(BF16)

    acc = jnp.dot(a_ref[...].astype(BF16), wb_ref[...], preferred_element_type=F32)
    if has_res:
        acc = r_ref[...] + acc
    o_ref[...] = acc.astype(o_ref.dtype)


def _matmul_ws(a, w, *, out_dtype, layer=None, n=None, res=None, tm=1024, tn=512, name="matmul_ws"):
    m, kdim = a.shape
    n = w.shape[-1] if n is None else n
    tm = _tile(m, tm)
    tn = _tile(n, tn, LANE)
    assert m % tm == 0 and n % tn == 0 and w.shape[-2] == kdim
    if layer is None:
        w_spec = pl.BlockSpec((kdim, tn), lambda j, i: (0, j))
    else:
        w_spec = pl.BlockSpec((None, kdim, tn), lambda j, i: (layer, 0, j))
    in_specs = [pl.BlockSpec((tm, kdim), lambda j, i: (i, 0)), w_spec]
    args = [a, w]
    if res is not None:
        in_specs.append(pl.BlockSpec((tm, tn), lambda j, i: (i, j)))
        args.append(res)
    return pl.pallas_call(
        functools.partial(_mm_ws_body, has_res=res is not None),
        grid=(n // tn, m // tm),
        in_specs=in_specs,
        out_specs=pl.BlockSpec((tm, tn), lambda j, i: (i, j)),
        out_shape=jax.ShapeDtypeStruct((m, n), out_dtype),
        scratch_shapes=[pltpu.VMEM((kdim, tn), BF16)],
        compiler_params=_params("parallel", "arbitrary"),
        name=name,
    )(*args)


def _silu_mul(h1, h3):
    return h1 * _sigmoid(h1) * h3


def _glu_body(a_ref, w1_ref, w3_ref, o_ref):
    a = a_ref[...]
    o_ref[...] = _silu_mul(_dot(a, w1_ref[...]), _dot(a, w3_ref[...])).astype(o_ref.dtype)


def _glu(a, w1, w3, *, tm=1024, tn=512, name="glu"):
    m, kdim = a.shape
    n = w1.shape[-1]
    tm = _tile(m, tm)
    tn = _tile(n, tn, LANE)
    assert m % tm == 0 and n % tn == 0
    return pl.pallas_call(
        _glu_body,
        grid=(m // tm, n // tn),
        in_specs=[pl.BlockSpec((tm, kdim), lambda i, j: (i, 0)),
                  pl.BlockSpec((kdim, tn), lambda i, j: (0, j)),
                  pl.BlockSpec((kdim, tn), lambda i, j: (0, j))],
        out_specs=pl.BlockSpec((tm, tn), lambda i, j: (i, j)),
        out_shape=jax.ShapeDtypeStruct((m, n), BF16),
        compiler_params=_params("parallel", "arbitrary"),
        name=name,
    )(a, w1, w3)


def _for_each_sub(n_live, live_fn, dead_fn):
    for s in range(MOE_NSUB):
        rows = slice(s * MOE_SUB, (s + 1) * MOE_SUB)
        pl.when(s < n_live)(functools.partial(live_fn, rows))
        pl.when(s >= n_live)(functools.partial(dead_fn, rows))


def _expert_glu_body(be_ref, ns_ref, a_ref, w1_ref, w3_ref, o_ref, w1b_ref, w3b_ref):
    del be_ref
    n_live = ns_ref[pl.program_id(0)]

    @pl.when(n_live > 0)
    def _():
        w1b_ref[...] = w1_ref[...].astype(BF16)
        w3b_ref[...] = w3_ref[...].astype(BF16)

    def live(rows):
        a = a_ref[rows, :]
        h1 = jnp.dot(a, w1b_ref[...], preferred_element_type=F32)
        h3 = jnp.dot(a, w3b_ref[...], preferred_element_type=F32)
        o_ref[rows, :] = _silu_mul(h1, h3).astype(o_ref.dtype)

    def dead(rows):
        o_ref[rows, :] = jnp.zeros((MOE_SUB, o_ref.shape[1]), o_ref.dtype)

    _for_each_sub(n_live, live, dead)


def _expert_glu(xs, w1, w3, layer, block_expert, n_sub, *, tn=256):
    m, kdim = xs.shape
    n = w1.shape[-1]
    tn = _tile(n, tn, LANE)
    nj = n // tn
    assert m % MOE_BLOCK == 0 and n % tn == 0

    def w_map(i, j, be, ns):
        return layer, be[i], 0, jnp.where(ns[i] > 0, j, nj - 1)

    gs = pltpu.PrefetchScalarGridSpec(
        num_scalar_prefetch=2,
        grid=(m // MOE_BLOCK, nj),
        in_specs=[pl.BlockSpec((MOE_BLOCK, kdim), lambda i, j, be, ns: (i, 0)),
                  pl.BlockSpec((None, None, kdim, tn), w_map),
                  pl.BlockSpec((None, None, kdim, tn), w_map)],
        out_specs=pl.BlockSpec((MOE_BLOCK, tn), lambda i, j, be, ns: (i, j)),
        scratch_shapes=[pltpu.VMEM((kdim, tn), BF16), pltpu.VMEM((kdim, tn), BF16)],
    )
    return pl.pallas_call(
        _expert_glu_body, grid_spec=gs, out_shape=jax.ShapeDtypeStruct((m, n), BF16),
        compiler_params=_params("parallel", "arbitrary"), name="expert_glu",
    )(block_expert, n_sub, xs, w1, w3)


def _expert_down_body(be_ref, ns_ref, a_ref, w_ref, o_ref, wb_ref, acc_ref, *, nk):
    del be_ref
    n_live = ns_ref[pl.program_id(0)]
    k = pl.program_id(2)

    @pl.when(n_live > 0)
    def _():
        wb_ref[...] = w_ref[...].astype(BF16)

    def live(rows):
        part = jnp.dot(a_ref[rows, :], wb_ref[...], preferred_element_type=F32)

        @pl.when(k == 0)
        def _():
            acc_ref[rows, :] = part

        @pl.when(k > 0)
        def _():
            acc_ref[rows, :] += part

        @pl.when(k == nk - 1)
        def _():
            o_ref[rows, :] = acc_ref[rows, :]

    def dead(rows):
        @pl.when(k == nk - 1)
        def _():
            o_ref[rows, :] = jnp.zeros((MOE_SUB, o_ref.shape[1]), o_ref.dtype)

    _for_each_sub(n_live, live, dead)


def _expert_down(hmid, w2, layer, block_expert, n_sub, *, tn=1024, tk=2048):
    m, kdim = hmid.shape
    n = w2.shape[-1]
    tn = _tile(n, tn, LANE)
    tk = _tile(kdim, tk, LANE)
    nj, nk = n // tn, kdim // tk
    assert m % MOE_BLOCK == 0 and n % tn == 0 and kdim % tk == 0

    def w_map(i, j, k, be, ns):
        live = ns[i] > 0
        return layer, be[i], jnp.where(live, k, nk - 1), jnp.where(live, j, nj - 1)

    gs = pltpu.PrefetchScalarGridSpec(
        num_scalar_prefetch=2,
        grid=(m // MOE_BLOCK, nj, nk),
        in_specs=[pl.BlockSpec((MOE_BLOCK, tk), lambda i, j, k, be, ns: (i, k)),
                  pl.BlockSpec((None, None, tk, tn), w_map)],
        out_specs=pl.BlockSpec((MOE_BLOCK, tn), lambda i, j, k, be, ns: (i, j)),
        scratch_shapes=[pltpu.VMEM((tk, tn), BF16), pltpu.VMEM((MOE_BLOCK, tn), F32)],
    )
    return pl.pallas_call(
        functools.partial(_expert_down_body, nk=nk), grid_spec=gs,
        out_shape=jax.ShapeDtypeStruct((m, n), F32),
        compiler_params=_params("parallel", "parallel", "arbitrary"), name="expert_down",
    )(block_expert, n_sub, hmid, w2)


def _logf_body(a_ref, w_ref, b_ref, o_ref):
    z = _dot(a_ref[...], w_ref[...]) + b_ref[...]
    o_ref[...] = jnp.minimum(z, 0.0) - jnp.log1p(jnp.exp(-jnp.abs(z)))


def _logf(xn, w_f, b_f):
    t, d = xn.shape
    nh = w_f.shape[1]
    wpad = jnp.zeros((d, LANE), BF16).at[:, :nh].set(w_f.astype(BF16))
    bpad = jnp.zeros((1, LANE), F32).at[0, :nh].set(b_f.astype(F32))
    tr = _tile(t, 1024)
    out = pl.pallas_call(
        _logf_body,
        grid=(t // tr,),
        in_specs=[pl.BlockSpec((tr, d), lambda i: (i, 0)), pl.BlockSpec((d, LANE), lambda i: (0, 0)),
                  pl.BlockSpec((1, LANE), lambda i: (0, 0))],
        out_specs=pl.BlockSpec((tr, LANE), lambda i: (i, 0)),
        out_shape=jax.ShapeDtypeStruct((t, LANE), F32),
        compiler_params=_params("parallel"),
        name="logf",
    )(xn, wpad, bpad)
    return out[:, :nh]


def _cumsum_body(x_ref, o_ref, *, nchunk):
    r = lax.broadcasted_iota(jnp.int32, (LANE, LANE), 0)
    c = lax.broadcasted_iota(jnp.int32, (LANE, LANE), 1)
    tri = (r <= c).astype(BF16)
    rows = x_ref.shape[1]

    def step(i, carry):
        h, m, l = _split3(x_ref[i])
        cs = (jnp.dot(h, tri, preferred_element_type=F32) + jnp.dot(m, tri, preferred_element_type=F32)
              + jnp.dot(l, tri, preferred_element_type=F32)) + carry
        o_ref[i] = cs
        return cs[:, LANE - 1:LANE]

    lax.fori_loop(0, nchunk, step, jnp.zeros((rows, 1), F32))


def _cumsum_time(x):
    b, l, nh = x.shape
    lp = -(-l // LANE) * LANE
    nchunk = lp // LANE
    rows = b * nh
    rp = -(-rows // 8) * 8
    xt = jnp.transpose(x, (0, 2, 1)).reshape(rows, l)
    xt = jnp.pad(xt, ((0, rp - rows), (0, lp - l)))
    xc = jnp.transpose(xt.reshape(rp, nchunk, LANE), (1, 0, 2))
    out = pl.pallas_call(
        functools.partial(_cumsum_body, nchunk=nchunk),
        out_shape=jax.ShapeDtypeStruct((nchunk, rp, LANE), F32),
        compiler_params=pltpu.CompilerParams(vmem_limit_bytes=V7X_VMEM_LIMIT),
        name="cumsum_time",
    )(xc)
    out = jnp.transpose(out, (1, 0, 2)).reshape(rp, lp)[:rows, :l]
    return out.reshape(b, nh, l)


def _fox_update(q, k, v, cq, ck, mask, m_ref, l_ref, acc_ref, hd, dh, scale):
    s = _dot_nt(q, k) * scale
    s = s + cq - ck
    if mask is not None:
        s = jnp.where(mask, s, NEG_INF)
    m_prev = m_ref[hd]
    m_new = jnp.maximum(m_prev, jnp.max(s, axis=-1, keepdims=True))
    alpha = jnp.exp(m_prev - m_new)
    p = jnp.exp(s - m_new)
    l_ref[hd] = alpha * l_ref[hd] + jnp.sum(p, axis=-1, keepdims=True)
    cols = slice(hd * dh, (hd + 1) * dh)
    acc_ref[:, cols] = alpha * acc_ref[:, cols] + _dot(p, v)
    m_ref[hd] = m_new


def _fox_init(m_ref, l_ref, acc_ref):
    m_ref[...] = jnp.full(m_ref.shape, NEG_INF, F32)
    l_ref[...] = jnp.zeros(l_ref.shape, F32)
    acc_ref[...] = jnp.zeros(acc_ref.shape, F32)


def _fox_finish(o_ref, m_ref, l_ref, acc_ref, nh, dh):
    for hd in range(nh):
        cols = slice(hd * dh, (hd + 1) * dh)
        o_ref[:, cols] = (acc_ref[:, cols] / l_ref[hd]).astype(o_ref.dtype)


def _fox_prompt_body(q_ref, k_ref, v_ref, cq_ref, ck_ref, o_ref, m_ref, l_ref, acc_ref, *, nh, dh, tq, nkv):
    qi = pl.program_id(1)
    kj = pl.program_id(2)

    @pl.when(kj == 0)
    def _():
        _fox_init(m_ref, l_ref, acc_ref)

    def update(mask):
        for hd in range(nh):
            cols = slice(hd * dh, (hd + 1) * dh)
            _fox_update(q_ref[:, cols], k_ref[:, cols], v_ref[:, cols], cq_ref[hd], ck_ref[hd], mask,
                        m_ref, l_ref, acc_ref, hd, dh, dh ** -0.5)

    @pl.when(kj < qi)
    def _():
        update(None)

    @pl.when(kj == qi)
    def _():
        update(lax.broadcasted_iota(jnp.int32, (tq, tq), 1) <= lax.broadcasted_iota(jnp.int32, (tq, tq), 0))

    @pl.when(kj == nkv - 1)
    def _():
        _fox_finish(o_ref, m_ref, l_ref, acc_ref, nh, dh)


def _fox_prompt(h, csum, *, nb, seq, nh, dh, col_q, col_k, col_v):
    fw = nh * dh
    tq = _tile(seq, 512)
    nq = seq // tq
    assert col_q % fw == 0 and col_k % fw == 0 and col_v % fw == 0
    cq = csum.reshape(nb, nh, seq, 1)
    ck = csum.reshape(nb, nh, 1, seq)
    body = functools.partial(_fox_prompt_body, nh=nh, dh=dh, tq=tq, nkv=nq)
    return pl.pallas_call(
        body,
        grid=(nb, nq, nq),
        in_specs=[
            pl.BlockSpec((tq, fw), lambda b, i, j: (b * nq + i, col_q // fw)),
            pl.BlockSpec((tq, fw), lambda b, i, j: (b * nq + jnp.minimum(i, j), col_k // fw)),
            pl.BlockSpec((tq, fw), lambda b, i, j: (b * nq + jnp.minimum(i, j), col_v // fw)),
            pl.BlockSpec((None, nh, tq, 1), lambda b, i, j: (b, 0, i, 0)),
            pl.BlockSpec((None, nh, 1, tq), lambda b, i, j: (b, 0, 0, jnp.minimum(i, j))),
        ],
        out_specs=pl.BlockSpec((tq, fw), lambda b, i, j: (b * nq + i, 0)),
        out_shape=jax.ShapeDtypeStruct((nb * seq, fw), BF16),
        scratch_shapes=[pltpu.VMEM((nh, tq, 1), F32), pltpu.VMEM((nh, tq, 1), F32), pltpu.VMEM((tq, fw), F32)],
        compiler_params=_params("parallel", "parallel", "arbitrary"),
        name="fox_prompt",
    )(h, h, h, cq, ck)


def _fox_sample_body(q_ref, kn_ref, vn_ref, kp_ref, vp_ref, cq_ref, ckp_ref, ckn_ref, o_ref,
                     m_ref, l_ref, acc_ref, *, nh, dh, lq, npast, tkp):
    kj = pl.program_id(1)
    scale = dh ** -0.5

    @pl.when(kj == 0)
    def _():
        _fox_init(m_ref, l_ref, acc_ref)

    @pl.when(kj < npast)
    def _():
        for hd in range(nh):
            cols = slice(hd * dh, (hd + 1) * dh)
            head_rows = pl.ds(hd, tkp, stride=nh)
            _fox_update(q_ref[:, cols], kp_ref[head_rows, :], vp_ref[head_rows, :], cq_ref[hd], ckp_ref[hd], None,
                        m_ref, l_ref, acc_ref, hd, dh, scale)

    @pl.when(kj == npast)
    def _():
        row = lax.broadcasted_iota(jnp.int32, (lq, lq), 0)
        col = lax.broadcasted_iota(jnp.int32, (lq, lq), 1)
        mask = col <= row
        for hd in range(nh):
            cols = slice(hd * dh, (hd + 1) * dh)
            _fox_update(q_ref[:, cols], kn_ref[:, cols], vn_ref[:, cols], cq_ref[hd], ckn_ref[hd], mask,
                        m_ref, l_ref, acc_ref, hd, dh, scale)
        _fox_finish(o_ref, m_ref, l_ref, acc_ref, nh, dh)


def _fox_sample(h, past_k, past_v, layer, csum, *, row0, nb, lq, nh, dh, col_q, col_k, col_v):
    fw = nh * dh
    depth, _, plen = past_k.shape[:3]
    tkp = _tile(plen, 1024)
    npast = plen // tkp
    assert row0 % lq == 0 and col_q % fw == 0 and col_k % fw == 0 and col_v % fw == 0
    rb = row0 // lq
    cq = csum[:, :, plen:].reshape(nb, nh, lq, 1)
    ckp = csum[:, :, :plen].reshape(nb, nh, 1, plen)
    ckn = csum[:, :, plen:].reshape(nb, nh, 1, lq)
    past_k = past_k.reshape(depth, nb, plen * nh, dh)
    past_v = past_v.reshape(depth, nb, plen * nh, dh)
    body = functools.partial(_fox_sample_body, nh=nh, dh=dh, lq=lq, npast=npast, tkp=tkp)
    pmap = lambda b, j: (layer, b, jnp.minimum(j, npast - 1), 0)
    return pl.pallas_call(
        body,
        grid=(nb, npast + 1),
        in_specs=[
            pl.BlockSpec((lq, fw), lambda b, j: (rb + b, col_q // fw)),
            pl.BlockSpec((lq, fw), lambda b, j: (rb + b, col_k // fw)),
            pl.BlockSpec((lq, fw), lambda b, j: (rb + b, col_v // fw)),
            pl.BlockSpec((None, None, tkp * nh, dh), pmap),
            pl.BlockSpec((None, None, tkp * nh, dh), pmap),
            pl.BlockSpec((None, nh, lq, 1), lambda b, j: (b, 0, 0, 0)),
            pl.BlockSpec((None, nh, 1, tkp), lambda b, j: (b, 0, 0, jnp.minimum(j, npast - 1))),
            pl.BlockSpec((None, nh, 1, lq), lambda b, j: (b, 0, 0, 0)),
        ],
        out_specs=pl.BlockSpec((lq, fw), lambda b, j: (b, 0)),
        out_shape=jax.ShapeDtypeStruct((nb * lq, fw), BF16),
        scratch_shapes=[pltpu.VMEM((nh, lq, 1), F32), pltpu.VMEM((nh, lq, 1), F32), pltpu.VMEM((lq, fw), F32)],
        compiler_params=_params("parallel", "arbitrary"),
        name="fox_sample",
    )(h, h, h, past_k, past_v, cq, ckp, ckn)


def _pool_body(u_ref, prev_ref, hist_ref, w_ref, sc_ref, o_ref, ext_ref, *, tl, gd, start_pos):
    i = pl.program_id(1)
    first = i == 0
    ext_ref[0:HALO, :] = jnp.where(first, hist_ref[...], prev_ref[...])
    ext_ref[HALO:HALO + tl, :] = u_ref[...]
    pos = start_pos + i * tl + lax.broadcasted_iota(jnp.int32, (tl, 1), 0)
    for g, w in enumerate(POOL_WINDOWS):
        cols = slice(g * gd, (g + 1) * gd)
        cur = ext_ref[HALO:HALO + tl, cols]
        win = cur
        for d in range(1, w):
            win = win + ext_ref[HALO - d:HALO - d + tl, cols]
        cnt = jnp.minimum(pos + 1, w).astype(F32)
        pooled = win / cnt - cur
        y = _dot(pooled, w_ref[g]) * sc_ref[:, cols]
        o_ref[:, cols] = y.astype(o_ref.dtype)


def _pool(h, hist, w_pool, pool_scale, *, row0, nb, seq, col_u, start_pos):
    ng, gd, _ = w_pool.shape
    c = ng * gd
    nhist = hist.shape[1]
    assert nhist < HALO and max(POOL_WINDOWS) - 1 <= nhist and ng == len(POOL_WINDOWS)
    tl = _tile(seq, 512, HALO)
    nl = seq // tl
    assert col_u % c == 0 and row0 % tl == 0 and tl % HALO == 0 and seq % HALO == 0
    rb = row0 // tl
    hb = tl // HALO
    hist_p = jnp.concatenate([jnp.zeros((nb, HALO - nhist, c), F32), hist.astype(F32)], axis=1)
    body = functools.partial(_pool_body, tl=tl, gd=gd, start_pos=start_pos)
    return pl.pallas_call(
        body,
        grid=(nb, nl),
        in_specs=[
            pl.BlockSpec((tl, c), lambda b, i: (rb + b * nl + i, col_u // c)),
            pl.BlockSpec((HALO, c), lambda b, i: (jnp.maximum((rb + b * nl + i) * hb - 1, 0), col_u // c)),
            pl.BlockSpec((None, HALO, c), lambda b, i: (b, 0, 0)),
            pl.BlockSpec((ng, gd, gd), lambda b, i: (0, 0, 0)),
            pl.BlockSpec((1, c), lambda b, i: (0, 0)),
        ],
        out_specs=pl.BlockSpec((tl, c), lambda b, i: (b * nl + i, 0)),
        out_shape=jax.ShapeDtypeStruct((nb * seq, c), BF16),
        scratch_shapes=[pltpu.VMEM((HALO + tl, c), F32)],
        compiler_params=_params("parallel", "arbitrary"),
        name="pool",
    )(h, h, hist_p, w_pool.astype(BF16), pool_scale.reshape(1, c).astype(F32))


def _ret_body(q_ref, k_ref, v_ref, g_ref, cos_ref, sin_ref, ld_ref, s0_ref, o_ref, sout_ref, s_ref, *, ch, dk, nchunk):
    n = pl.program_id(2)

    @pl.when(n == 0)
    def _():
        s_ref[...] = s0_ref[...]

    ld = ld_ref[:, 0:1]
    cos_t = cos_ref[...]
    sin_t = sin_ref[...]

    def rope(x):
        return x * cos_t + pltpu.roll(x, dk // 2, axis=1) * sin_t

    q = rope(q_ref[...])
    k = rope(k_ref[...]) * (dk ** -0.5)
    v = v_ref[...]
    ii = lax.broadcasted_iota(jnp.int32, (ch, 1), 0).astype(F32)
    diff = ii - lax.broadcasted_iota(jnp.int32, (1, ch), 1).astype(F32)
    dmask = jnp.where(diff >= 0, jnp.exp(ld * jnp.maximum(diff, 0.0)), 0.0)
    scores = _dot_nt(q, k) * dmask
    o = _dot(scores, v)
    s_prev = s_ref[...]
    o = o + _dot(q * jnp.exp(ld * (ii + 1.0)), s_prev)
    k_dec = k * jnp.exp(ld * (ch - 1.0 - ii))
    s_ref[...] = jnp.exp(ld * ch) * s_prev + _dot_tn(k_dec, v)
    o = o * lax.rsqrt(jnp.mean(o * o, axis=-1, keepdims=True) + EPS)
    gate = g_ref[...]
    o_ref[...] = (o * (gate * _sigmoid(gate))).astype(o_ref.dtype)

    @pl.when(n == nchunk - 1)
    def _():
        sout_ref[...] = s_ref[...]


def _retention(h, state0, *, row0, nb, seq, col_q, col_k, col_v, col_g, start_pos):
    _, nh, dk, dv = state0.shape
    ch = _tile(seq, 256)
    nchunk = seq // ch
    assert row0 % ch == 0 and col_q % dk == 0 and col_k % dk == 0 and col_v % dv == 0 and col_g % dv == 0
    rb = row0 // ch
    half = dk // 2
    inv = ROPE_BASE ** (-jnp.arange(half, dtype=F32) / half)
    ang = (start_pos + jnp.arange(seq, dtype=jnp.int32)).astype(F32)[:, None] * inv[None, :]
    cos_t = jnp.concatenate([jnp.cos(ang), jnp.cos(ang)], axis=1)
    sin_t = jnp.concatenate([-jnp.sin(ang), jnp.sin(ang)], axis=1)
    ld = jnp.log1p(-jnp.exp2(-5.0 - jnp.arange(nh, dtype=F32)))
    ldv = jnp.broadcast_to(ld[:, None, None], (nh, 1, LANE))
    body = functools.partial(_ret_body, ch=ch, dk=dk, nchunk=nchunk)
    rows = lambda b, hd, n: rb + b * nchunk + n
    return pl.pallas_call(
        body,
        grid=(nb, nh, nchunk),
        in_specs=[
            pl.BlockSpec((ch, dk), lambda b, hd, n: (rows(b, hd, n), col_q // dk + hd)),
            pl.BlockSpec((ch, dk), lambda b, hd, n: (rows(b, hd, n), col_k // dk + hd)),
            pl.BlockSpec((ch, dv), lambda b, hd, n: (rows(b, hd, n), col_v // dv + hd)),
            pl.BlockSpec((ch, dv), lambda b, hd, n: (rows(b, hd, n), col_g // dv + hd)),
            pl.BlockSpec((ch, dk), lambda b, hd, n: (n, 0)),
            pl.BlockSpec((ch, dk), lambda b, hd, n: (n, 0)),
            pl.BlockSpec((None, 1, LANE), lambda b, hd, n: (hd, 0, 0)),
            pl.BlockSpec((None, None, dk, dv), lambda b, hd, n: (b, hd, 0, 0)),
        ],
        out_specs=[
            pl.BlockSpec((ch, dv), lambda b, hd, n: (b * nchunk + n, hd)),
            pl.BlockSpec((None, None, dk, dv), lambda b, hd, n: (b, hd, 0, 0)),
        ],
        out_shape=[jax.ShapeDtypeStruct((nb * seq, nh * dv), BF16),
                   jax.ShapeDtypeStruct((nb, nh, dk, dv), F32)],
        scratch_shapes=[pltpu.VMEM((dk, dv), F32)],
        compiler_params=_params("parallel", "parallel", "arbitrary"),
        name="retention",
    )(h, h, h, h, cos_t, sin_t, ldv, state0.astype(F32))


def _merge_body(fo_ref, po_ref, ro_ref, wf_ref, wp_ref, wr_ref, g0_ref, g1_ref, g2_ref, o_ref):
    acc = _sigmoid(g0_ref[...]) * _dot(fo_ref[...], wf_ref[...])
    acc = acc + _sigmoid(g1_ref[...]) * _dot(po_ref[...], wp_ref[...])
    acc = acc + _sigmoid(g2_ref[...]) * _dot(ro_ref[...], wr_ref[...])
    o_ref[...] = acc.astype(o_ref.dtype)


def _merge(fo, po, ro, wf, wp, wr, h, *, col_g, d):
    t = fo.shape[0]
    tm = _tile(t, 1024)
    tn = _tile(math.gcd(d, col_g), 512, LANE)
    gb = col_g // tn
    nd = d // tn
    kf, kp, kr = fo.shape[1], po.shape[1], ro.shape[1]
    return pl.pallas_call(
        _merge_body,
        grid=(t // tm, nd),
        in_specs=[
            pl.BlockSpec((tm, kf), lambda i, j: (i, 0)),
            pl.BlockSpec((tm, kp), lambda i, j: (i, 0)),
            pl.BlockSpec((tm, kr), lambda i, j: (i, 0)),
            pl.BlockSpec((kf, tn), lambda i, j: (0, j)),
            pl.BlockSpec((kp, tn), lambda i, j: (0, j)),
            pl.BlockSpec((kr, tn), lambda i, j: (0, j)),
            pl.BlockSpec((tm, tn), lambda i, j: (i, gb + j)),
            pl.BlockSpec((tm, tn), lambda i, j: (i, gb + nd + j)),
            pl.BlockSpec((tm, tn), lambda i, j: (i, gb + 2 * nd + j)),
        ],
        out_specs=pl.BlockSpec((tm, tn), lambda i, j: (i, j)),
        out_shape=jax.ShapeDtypeStruct((t, d), BF16),
        compiler_params=_params("parallel", "arbitrary"),
        name="merge",
    )(fo, po, ro, wf, wp, wr, h, h, h)


def _xattn_body(q_ref, k_ref, v_ref, o_ref, *, nh, dh):
    scale = dh ** -0.5
    for hd in range(nh):
        cols = slice(hd * dh, (hd + 1) * dh)
        s = _dot_nt(q_ref[:, cols], k_ref[:, cols]) * scale
        p = jnp.exp(s - jnp.max(s, axis=-1, keepdims=True))
        o = _dot(p, v_ref[:, cols]) / jnp.sum(p, axis=-1, keepdims=True)
        o_ref[:, cols] = o.astype(o_ref.dtype)


def _xattn(q, mk, mv, *, row0, nb, seq, nh, dh):
    w = nh * dh
    mt = mk.shape[1]
    tq = _tile(seq, 512)
    nq = seq // tq
    assert row0 % tq == 0
    rb = row0 // tq
    return pl.pallas_call(
        functools.partial(_xattn_body, nh=nh, dh=dh),
        grid=(nb, nq),
        in_specs=[
            pl.BlockSpec((tq, w), lambda b, i: (rb + b * nq + i, 0)),
            pl.BlockSpec((None, mt, w), lambda b, i: (b, 0, 0)),
            pl.BlockSpec((None, mt, w), lambda b, i: (b, 0, 0)),
        ],
        out_specs=pl.BlockSpec((tq, w), lambda b, i: (b * nq + i, 0)),
        out_shape=jax.ShapeDtypeStruct((nb * seq, w), BF16),
        compiler_params=_params("parallel", "arbitrary"),
        name="xattn",
    )(q, mk, mv)


def _rank_body(sel_ref, rk_ref, cnt_ref, carry_ref, *, tb):
    i = pl.program_id(0)

    @pl.when(i == 0)
    def _():
        carry_ref[...] = jnp.zeros_like(carry_ref)

    sel = sel_ref[...]
    lane = lax.broadcasted_iota(jnp.int32, sel.shape, 1)
    lanef = lane.astype(F32)
    e1 = sel[:, 0:1]
    e2 = sel[:, 1:2]
    onehot = jnp.logical_or(lanef == e1, lanef == e2).astype(F32)
    r = lax.broadcasted_iota(jnp.int32, (tb, tb), 0)
    c = lax.broadcasted_iota(jnp.int32, (tb, tb), 1)
    before = (c < r).astype(BF16)
    rank = jnp.dot(before, onehot.astype(BF16), preferred_element_type=F32) + carry_ref[...]
    r1 = jnp.sum(jnp.where(lanef == e1, rank, 0.0), axis=-1, keepdims=True)
    r2 = jnp.sum(jnp.where(lanef == e2, rank, 0.0), axis=-1, keepdims=True)
    rk_ref[...] = jnp.where(lane == 0, r1, jnp.where(lane == 1, r2, 0.0))
    total = carry_ref[...] + jnp.sum(onehot, axis=0, keepdims=True)
    carry_ref[...] = total
    cnt_ref[...] = total


def _expert_ranks(sel):
    t = sel.shape[0]
    tb = _tile(t, 512)
    return pl.pallas_call(
        functools.partial(_rank_body, tb=tb),
        grid=(t // tb,),
        in_specs=[pl.BlockSpec((tb, LANE), lambda i: (i, 0))],
        out_specs=[pl.BlockSpec((tb, LANE), lambda i: (i, 0)), pl.BlockSpec((1, LANE), lambda i: (0, 0))],
        out_shape=[jax.ShapeDtypeStruct((t, LANE), F32), jax.ShapeDtypeStruct((1, LANE), F32)],
        scratch_shapes=[pltpu.VMEM((1, LANE), F32)],
        compiler_params=_params("arbitrary"),
        name="expert_ranks",
    )(sel)


def _row_copy(src, dst, sem, s, d):
    return pltpu.make_async_copy(src.at[s], dst.at[d], sem)


def _dispatch_body(tok_ref, x_ref, o_ref, sem, *, tb):
    def issue(r, c):
        _row_copy(x_ref, o_ref, sem, tok_ref[0, 0, r], r).start()
        return c

    lax.fori_loop(0, tb, issue, 0)

    def drain(r, c):
        _row_copy(x_ref, o_ref, sem, 0, 0).wait()
        return c

    lax.fori_loop(0, tb, drain, 0)


def _dispatch(x, slot_tok):
    t, d = x.shape
    n_slots = slot_tok.shape[0]
    tb = MOE_SUB
    s = d // LANE
    xs = pl.pallas_call(
        functools.partial(_dispatch_body, tb=tb),
        grid=(n_slots // tb,),
        in_specs=[pl.BlockSpec((1, 1, tb), lambda i: (i, 0, 0), memory_space=pltpu.SMEM),
                  pl.BlockSpec(memory_space=pl.ANY)],
        out_specs=pl.BlockSpec((tb, s, LANE), lambda i: (i, 0, 0)),
        out_shape=jax.ShapeDtypeStruct((n_slots, s, LANE), x.dtype),
        scratch_shapes=[pltpu.SemaphoreType.DMA(())],
        compiler_params=_params("arbitrary"),
        name="moe_dispatch",
    )(slot_tok.reshape(n_slots // tb, 1, tb), x.reshape(t, s, LANE))
    return xs.reshape(n_slots, d)


def _combine_body(dest_ref, g0_ref, g1_ref, x_ref, y_ref, o_ref, r0_ref, r1_ref, sem, *, tb):
    bufs = (r0_ref, r1_ref)

    def issue(r, c):
        for k in range(TOP_K):
            _row_copy(y_ref, bufs[k], sem, dest_ref[0, 0, TOP_K * r + k], r).start()
        return c

    lax.fori_loop(0, tb, issue, 0)

    def drain(r, c):
        _row_copy(y_ref, r0_ref, sem, 0, 0).wait()
        return c

    lax.fori_loop(0, TOP_K * tb, drain, 0)
    o_ref[...] = x_ref[...] + (r0_ref[...] * g0_ref[...] + r1_ref[...] * g1_ref[...])


def _combine(x, yb, dest, sel):
    t, d = x.shape
    tb = _tile(t, 256)
    s = d // LANE
    dest3 = dest.reshape(t // tb, 1, TOP_K * tb)
    g0 = sel[:, 2].reshape(t, 1, 1)
    g1 = sel[:, 3].reshape(t, 1, 1)
    row = pl.BlockSpec((tb, s, LANE), lambda i: (i, 0, 0))
    gate = pl.BlockSpec((tb, 1, 1), lambda i: (i, 0, 0))
    out = pl.pallas_call(
        functools.partial(_combine_body, tb=tb),
        grid=(t // tb,),
        in_specs=[pl.BlockSpec((1, 1, TOP_K * tb), lambda i: (i, 0, 0), memory_space=pltpu.SMEM),
                  gate, gate, row, pl.BlockSpec(memory_space=pl.ANY)],
        out_specs=row,
        out_shape=jax.ShapeDtypeStruct((t, s, LANE), F32),
        scratch_shapes=[pltpu.VMEM((tb, s, LANE), F32), pltpu.VMEM((tb, s, LANE), F32),
                        pltpu.SemaphoreType.DMA(())],
        compiler_params=_params("arbitrary"),
        name="moe_combine",
    )(dest3, g0, g1, x.reshape(t, s, LANE), yb.reshape(-1, s, LANE))
    return out.reshape(t, d)


def _moe_ffn(x, g, router, w1, w3, w2, layer):
    t, d = x.shape
    n_exp = router.shape[1]
    xn, sel = _rmsnorm_router(x, g, router)
    rk, cnt = _expert_ranks(sel)
    counts = cnt[0, :n_exp].astype(jnp.int32)
    region = ((counts + MOE_BLOCK - 1) // MOE_BLOCK) * MOE_BLOCK
    region_end = jnp.cumsum(region)
    region_start = region_end - region
    n_blocks = -(-(t * TOP_K) // MOE_BLOCK) + n_exp
    n_slots = n_blocks * MOE_BLOCK
    eid = sel[:, :TOP_K].astype(jnp.int32)
    dest = (region_start[eid] + rk[:, :TOP_K].astype(jnp.int32)).astype(jnp.int32)
    block_start = jnp.arange(n_blocks, dtype=jnp.int32) * MOE_BLOCK
    block_exp = jnp.minimum(jnp.sum(block_start[:, None] >= region_end[None, :], axis=1), n_exp - 1).astype(jnp.int32)
    used = block_start < region_end[-1]
    rows_left = counts[block_exp] - (block_start - region_start[block_exp])
    n_sub = jnp.where(used, jnp.clip((rows_left + MOE_SUB - 1) // MOE_SUB, 0, MOE_NSUB), 0).astype(jnp.int32)
    last_used = jnp.maximum(region_end[-1] // MOE_BLOCK - 1, 0)
    block_exp = jnp.where(used, block_exp, block_exp[last_used]).astype(jnp.int32)
    tok = jnp.repeat(jnp.arange(t, dtype=jnp.int32), TOP_K)
    slot_tok = jnp.zeros((n_slots,), jnp.int32).at[dest.reshape(-1)].set(tok)
    xs = _dispatch(xn, slot_tok)
    hmid = _expert_glu(xs, w1, w3, layer, block_exp, n_sub)
    yb = _expert_down(hmid, w2, layer, block_exp, n_sub)
    return _combine(x, yb, dest, sel)


def _pad_cols(w, mult):
    n = w.shape[-1]
    return jnp.pad(w, [(0, 0)] * (w.ndim - 1) + [(0, -(-n // mult) * mult - n)])


def _pad_rows(w, mult):
    n = w.shape[-2]
    return jnp.pad(w, [(0, 0)] * (w.ndim - 2) + [(0, -(-n // mult) * mult - n), (0, 0)])


def kernel(x_prompt, x_sample, mem_prompt, cache_fox_k, cache_fox_v, cache_fox_logf, state_pool, state_ret, cache_mem_k, cache_mem_v, norm_mix_g, w_in, b_forget, w_pool, pool_scale, w_br_fox, w_br_pool, w_br_ret, w_out, norm_x_g, norm_mem_g, w_xq, w_xk, w_xv, w_xo, norm_ffn_g, ffn_w1, ffn_w3, ffn_w2, moe_router, moe_w1, moe_w3, moe_w2, norm_final_g):
    bp, sp, d = x_prompt.shape
    bs, ss, _ = x_sample.shape
    depth = w_in.shape[0]
    past = cache_fox_k.shape[2]
    nh_f, dh_f = cache_fox_k.shape[3], cache_fox_k.shape[4]
    fw = nh_f * dh_f
    nhist, pw = state_pool.shape[2], state_pool.shape[3]
    nh_r, dk_r, dv_r = state_ret.shape[2], state_ret.shape[3], state_ret.shape[4]
    rqk, rvw = nh_r * dk_r, nh_r * dv_r
    mt, nh_m, dh_m = cache_mem_k.shape[2], cache_mem_k.shape[3], cache_mem_k.shape[4]
    mw = nh_m * dh_m
    tp, ts = bp * sp, bs * ss
    assert sp >= nhist and ss >= nhist

    c_fq, c_fk, c_fv = 0, fw, 2 * fw
    c_pu = 0
    c_rq = c_pu + pw
    c_rk = c_rq + rqk
    c_rv = c_rk + rqk
    c_rg = c_rv + rvw
    c_g = c_rg + rvw
    assert w_in.shape[2] == 3 * fw + nh_f + c_g + 3 * d

    x = jnp.concatenate([x_prompt.reshape(tp, d), x_sample.reshape(ts, d)], axis=0).astype(F32)
    outs = {k: [] for k in ("pk", "pv", "plf", "pp", "pr", "pmk", "pmv", "sk", "sv", "slf", "spool", "sr")}

    for l in range(depth):
        w_a = w_in[l, :, :3 * fw].astype(BF16)
        w_flog = w_in[l, :, 3 * fw:3 * fw + nh_f]
        w_b = w_in[l, :, 3 * fw + nh_f:].astype(BF16)

        xn = _rmsnorm(x, norm_mix_g[l], BF16)
        ha = _matmul(xn, w_a, out_dtype=F32, name="proj_in_fox")
        hb = _matmul(xn, w_b, out_dtype=F32, name="proj_in")
        logf = _logf(xn, w_flog, b_forget[l])
        logf_p = logf[:tp].reshape(bp, sp, nh_f)
        logf_s = logf[tp:].reshape(bs, ss, nh_f)
        cs_p = _cumsum_time(logf_p)
        cs_s = _cumsum_time(jnp.concatenate([cache_fox_logf[l].astype(F32), logf_s], axis=1))
        fo_p = _fox_prompt(ha, cs_p, nb=bp, seq=sp, nh=nh_f, dh=dh_f, col_q=c_fq, col_k=c_fk, col_v=c_fv)
        fo_s = _fox_sample(ha, cache_fox_k, cache_fox_v, l, cs_s,
                           row0=tp, nb=bs, lq=ss, nh=nh_f, dh=dh_f, col_q=c_fq, col_k=c_fk, col_v=c_fv)
        fo = jnp.concatenate([fo_p, fo_s], axis=0)

        po_p = _pool(hb, jnp.zeros((bp, nhist, pw), F32), w_pool[l], pool_scale[l],
                     row0=0, nb=bp, seq=sp, col_u=c_pu, start_pos=0)
        po_s = _pool(hb, state_pool[l], w_pool[l], pool_scale[l],
                     row0=tp, nb=bs, seq=ss, col_u=c_pu, start_pos=past)
        po = jnp.concatenate([po_p, po_s], axis=0)

        ro_p, rs_p = _retention(hb, jnp.zeros((bp, nh_r, dk_r, dv_r), F32), row0=0, nb=bp, seq=sp,
                                col_q=c_rq, col_k=c_rk, col_v=c_rv, col_g=c_rg, start_pos=0)
        ro_s, rs_s = _retention(hb, state_ret[l], row0=tp, nb=bs, seq=ss,
                                col_q=c_rq, col_k=c_rk, col_v=c_rv, col_g=c_rg, start_pos=past)
        ro = jnp.concatenate([ro_p, ro_s], axis=0)

        merged = _merge(fo, po, ro, w_br_fox[l].astype(BF16), w_br_pool[l].astype(BF16),
                        w_br_ret[l].astype(BF16), hb, col_g=c_g, d=d)
        x = _matmul_ws(merged, w_out, layer=l, out_dtype=F32, res=x, name="proj_out")

        mn = _rmsnorm(mem_prompt.reshape(bp * mt, d).astype(F32), norm_mem_g[l], BF16)
        mk = _matmul_ws(mn, w_xk, layer=l, out_dtype=F32, name="mem_k")
        mv = _matmul_ws(mn, w_xv, layer=l, out_dtype=F32, name="mem_v")
        xn = _rmsnorm(x, norm_x_g[l], BF16)
        q = _matmul_ws(xn, w_xq, layer=l, out_dtype=BF16, name="xattn_q")
        xo_p = _xattn(q, mk.reshape(bp, mt, mw), mv.reshape(bp, mt, mw), row0=0, nb=bp, seq=sp, nh=nh_m, dh=dh_m)
        xo_s = _xattn(q, cache_mem_k[l].reshape(bs, mt, mw).astype(F32), cache_mem_v[l].reshape(bs, mt, mw).astype(F32),
                      row0=tp, nb=bs, seq=ss, nh=nh_m, dh=dh_m)
        x = _matmul_ws(jnp.concatenate([xo_p, xo_s], axis=0), w_xo, layer=l, out_dtype=F32, res=x,
                       name="xattn_out")

        j = l // 2
        if l % 2 == 0:
            xn = _rmsnorm(x, norm_ffn_g[l], BF16)
            w1 = _pad_cols(ffn_w1[j], 1024).astype(BF16)
            w3 = _pad_cols(ffn_w3[j], 1024).astype(BF16)
            w2 = _pad_rows(ffn_w2[j], 1024).astype(BF16)
            hmid = _glu(xn, w1, w3, name="ffn_glu")
            x = _matmul(hmid, w2, out_dtype=F32, res=x, tk=1024, name="ffn_down")
        else:
            x = _moe_ffn(x, norm_ffn_g[l], moe_router[j], moe_w1, moe_w3, moe_w2, j)

        outs["pk"].append(ha[:tp, c_fk:c_fk + fw].reshape(bp, sp, nh_f, dh_f))
        outs["pv"].append(ha[:tp, c_fv:c_fv + fw].reshape(bp, sp, nh_f, dh_f))
        outs["plf"].append(logf_p)
        outs["pp"].append(hb[:tp, c_pu:c_pu + pw].reshape(bp, sp, pw)[:, sp - nhist:])
        outs["pr"].append(rs_p)
        outs["pmk"].append(mk.reshape(bp, mt, nh_m, dh_m))
        outs["pmv"].append(mv.reshape(bp, mt, nh_m, dh_m))
        outs["sk"].append(ha[tp:, c_fk:c_fk + fw].reshape(bs, ss, nh_f, dh_f))
        outs["sv"].append(ha[tp:, c_fv:c_fv + fw].reshape(bs, ss, nh_f, dh_f))
        outs["slf"].append(logf_s)
        outs["spool"].append(hb[tp:, c_pu:c_pu + pw].reshape(bs, ss, pw)[:, ss - nhist:])
        outs["sr"].append(rs_s)

    y = _rmsnorm(x, norm_final_g, F32)
    st = {k: jnp.stack(v) for k, v in outs.items()}
    return (y[:tp].reshape(bp, sp, d), y[tp:].reshape(bs, ss, d),
            st["pk"], st["pv"], st["plf"], st["pp"], st["pr"], st["pmk"], st["pmv"],
            st["sk"], st["sv"], st["slf"], st["spool"], st["sr"])
```

```python
import functools
import math

import jax
import jax.numpy as jnp
from jax import lax
from jax.experimental import pallas as pl
from jax.experimental.pallas import tpu as pltpu

EPS = 1e-6
NEG_INF = -1e30
POOL_WINDOWS = (2, 4, 8, 16)
ROPE_BASE = 10000.0
TOP_K = 2

LANE = 128
HALO = 16
V7X_VMEM_LIMIT = 56 * 1024 * 1024
MOE_BLOCK = 1024
MOE_SUB = 256
MOE_NSUB = MOE_BLOCK // MOE_SUB

F32 = jnp.float32
BF16 = jnp.bfloat16


def _tile(n, pref, mult=8):
    if n <= pref:
        return n
    t = (pref // mult) * mult
    while t >= mult:
        if n % t == 0:
            return t
        t -= mult
    return n


def _params(*sem):
    return pltpu.CompilerParams(dimension_semantics=sem, vmem_limit_bytes=V7X_VMEM_LIMIT)


def _dot(a, b):
    return jnp.dot(a.astype(BF16), b.astype(BF16), preferred_element_type=F32)


def _dot_nt(a, b):
    return lax.dot_general(a.astype(BF16), b.astype(BF16), (((1,), (1,)), ((), ())), preferred_element_type=F32)


def _dot_tn(a, b):
    return lax.dot_general(a.astype(BF16), b.astype(BF16), (((0,), (0,)), ((), ())), preferred_element_type=F32)


def _sigmoid(x):
    return 1.0 / (1.0 + jnp.exp(-x))


def _rmsnorm_body(x_ref, g_ref, o_ref):
    x = x_ref[...]
    y = x * lax.rsqrt(jnp.mean(x * x, axis=-1, keepdims=True) + EPS)
    o_ref[...] = (y * g_ref[...]).astype(o_ref.dtype)


def _rmsnorm(x, g, out_dtype):
    t, d = x.shape
    tr = _tile(t, 256)
    return pl.pallas_call(
        _rmsnorm_body,
        grid=(t // tr,),
        in_specs=[pl.BlockSpec((tr, d), lambda i: (i, 0)), pl.BlockSpec((1, d), lambda i: (0, 0))],
        out_specs=pl.BlockSpec((tr, d), lambda i: (i, 0)),
        out_shape=jax.ShapeDtypeStruct((t, d), out_dtype),
        compiler_params=_params("parallel"),
        name="rmsnorm",
    )(x, g.reshape(1, d).astype(F32))


def _split3(x):
    h = x.astype(BF16)
    r = x - h.astype(F32)
    m = r.astype(BF16)
    l = (r - m.astype(F32)).astype(BF16)
    return h, m, l


def _rmsnorm_router_body(x_ref, g_ref, w_ref, o_ref, sel_ref, *, n_exp):
    x = x_ref[...]
    y = x * lax.rsqrt(jnp.mean(x * x, axis=-1, keepdims=True) + EPS) * g_ref[...]
    o_ref[...] = y.reshape(o_ref.shape).astype(o_ref.dtype)
    yh, ym, _ = _split3(y)
    w = w_ref[...]
    wh, wm, _ = _split3(w)
    logits = (jnp.dot(yh, wh, preferred_element_type=F32) + jnp.dot(yh, wm, preferred_element_type=F32)
              + jnp.dot(ym, wh, preferred_element_type=F32))
    lane = lax.broadcasted_iota(jnp.int32, logits.shape, 1)
    logits = jnp.where(lane < n_exp, logits, NEG_INF)
    m1 = jnp.max(logits, axis=-1, keepdims=True)
    i1 = jnp.min(jnp.where(logits == m1, lane, LANE), axis=-1, keepdims=True)
    rest = jnp.where(lane == i1, NEG_INF, logits)
    m2 = jnp.max(rest, axis=-1, keepdims=True)
    i2 = jnp.min(jnp.where(rest == m2, lane, LANE), axis=-1, keepdims=True)
    e2 = jnp.exp(m2 - m1)
    den = 1.0 + e2
    g1 = 1.0 / den
    g2 = e2 / den
    sel = jnp.where(lane == 0, i1.astype(F32),
                    jnp.where(lane == 1, i2.astype(F32),
                              jnp.where(lane == 2, g1, jnp.where(lane == 3, g2, 0.0))))
    sel_ref[...] = sel


def _rmsnorm_router(x, g, router):
    t, d = x.shape
    n_exp = router.shape[1]
    tr = _tile(t, 256)
    wpad = jnp.zeros((d, LANE), F32).at[:, :n_exp].set(router.astype(F32))
    return pl.pallas_call(
        functools.partial(_rmsnorm_router_body, n_exp=n_exp),
        grid=(t // tr,),
        in_specs=[pl.BlockSpec((tr, d), lambda i: (i, 0)), pl.BlockSpec((1, d), lambda i: (0, 0)),
                  pl.BlockSpec((d, LANE), lambda i: (0, 0))],
        out_specs=[pl.BlockSpec((tr, d // LANE, LANE), lambda i: (i, 0, 0)), pl.BlockSpec((tr, LANE), lambda i: (i, 0))],
        out_shape=[jax.ShapeDtypeStruct((t, d // LANE, LANE), BF16), jax.ShapeDtypeStruct((t, LANE), F32)],
        compiler_params=_params("parallel"),
        name="rmsnorm_router",
    )(x, g.reshape(1, d).astype(F32), wpad)


def _mm_body(a_ref, b_ref, *refs, nk, has_res):
    if has_res:
        r_ref, *refs = refs
    o_ref, *scr = refs

    def finish(acc):
        if has_res:
            acc = r_ref[...] + acc
        o_ref[...] = acc.astype(o_ref.dtype)

    part = _dot(a_ref[...], b_ref[...])
    if nk == 1:
        finish(part)
    else:
        k = pl.program_id(2)
        acc_ref = scr[0]

        @pl.when(k == 0)
        def _():
            acc_ref[...] = part

        @pl.when(k > 0)
        def _():
            acc_ref[...] += part

        @pl.when(k == nk - 1)
        def _():
            finish(acc_ref[...])


def _matmul(a, b, *, out_dtype, res=None, tm=1024, tn=1024, tk=4096, name="matmul"):
    m, kdim = a.shape
    n = b.shape[-1]
    tm = _tile(m, tm)
    tn = _tile(n, tn, LANE)
    tk = _tile(kdim, tk, LANE)
    nk = kdim // tk
    assert m % tm == 0 and n % tn == 0 and kdim % tk == 0
    in_specs = [pl.BlockSpec((tm, tk), lambda i, j, k: (i, k)), pl.BlockSpec((tk, tn), lambda i, j, k: (k, j))]
    args = [a, b]
    if res is not None:
        in_specs.append(pl.BlockSpec((tm, tn), lambda i, j, k: (i, j)))
        args.append(res)
    return pl.pallas_call(
        functools.partial(_mm_body, nk=nk, has_res=res is not None),
        grid=(m // tm, n // tn, nk),
        in_specs=in_specs,
        out_specs=pl.BlockSpec((tm, tn), lambda i, j, k: (i, j)),
        out_shape=jax.ShapeDtypeStruct((m, n), out_dtype),
        scratch_shapes=[pltpu.VMEM((tm, tn), F32)] if nk > 1 else [],
        compiler_params=_params("parallel", "parallel", "arbitrary"),
        name=name,
    )(*args)


def _mm_ws_body(a_ref, w_ref, *refs, has_res):
    if has_res:
        r_ref, *refs = refs
    o_ref, wb_ref = refs

    @pl.when(pl.program_id(1) == 0)
    def _():
        wb_ref[...] = w_ref[...].astype(BF16)

    acc = jnp.dot(a_ref[...].astype(BF16), wb_ref[...], preferred_element_type=F32)
    if has_res:
        acc = r_ref[...] + acc
    o_ref[...] = acc.astype(o_ref.dtype)


def _matmul_ws(a, w, *, out_dtype, layer=None, n=None, res=None, tm=1024, tn=512, name="matmul_ws"):
    m, kdim = a.shape
    n = w.shape[-1] if n is None else n
    tm = _tile(m, tm)
    tn = _tile(n, tn, LANE)
    assert m % tm == 0 and n % tn == 0 and w.shape[-2] == kdim
    if layer is None:
        w_spec = pl.BlockSpec((kdim, tn), lambda j, i: (0, j))
    else:
        w_spec = pl.BlockSpec((None, kdim, tn), lambda j, i: (layer, 0, j))
    in_specs = [pl.BlockSpec((tm, kdim), lambda j, i: (i, 0)), w_spec]
    args = [a, w]
    if res is not None:
        in_specs.append(pl.BlockSpec((tm, tn), lambda j, i: (i, j)))
        args.append(res)
    return pl.pallas_call(
        functools.partial(_mm_ws_body, has_res=res is not None),
        grid=(n // tn, m // tm),
        in_specs=in_specs,
        out_specs=pl.BlockSpec((tm, tn), lambda j, i: (i, j)),
        out_shape=jax.ShapeDtypeStruct((m, n), out_dtype),
        scratch_shapes=[pltpu.VMEM((kdim, tn), BF16)],
        compiler_params=_params("parallel", "arbitrary"),
        name=name,
    )(*args)


def _mm_wst_body(a_ref, wt_ref, o_ref, wb_ref):
    @pl.when(pl.program_id(1) == 0)
    def _():
        wb_ref[...] = wt_ref[...].T.astype(BF16)

    o_ref[...] = jnp.dot(a_ref[...].astype(BF16), wb_ref[...], preferred_element_type=F32).astype(o_ref.dtype)


def _matmul_wst(a, wt, *, row0, n, out_dtype, tm=1024, tn=512, name="matmul_wst"):
    m, kdim = a.shape
    tm = _tile(m, tm)
    tn = _tile(n, tn, LANE)
    assert m % tm == 0 and n % tn == 0 and wt.shape[1] == kdim and row0 % 8 == 0
    return pl.pallas_call(
        _mm_wst_body,
        grid=(n // tn, m // tm),
        in_specs=[pl.BlockSpec((tm, kdim), lambda j, i: (i, 0)),
                  pl.BlockSpec((pl.Element(tn), pl.Element(kdim)),
                               lambda j, i: (pl.multiple_of(row0 + j * tn, 8), 0))],
        out_specs=pl.BlockSpec((tm, tn), lambda j, i: (i, j)),
        out_shape=jax.ShapeDtypeStruct((m, n), out_dtype),
        scratch_shapes=[pltpu.VMEM((kdim, tn), BF16)],
        compiler_params=_params("parallel", "arbitrary"),
        name=name,
    )(a, wt)


def _silu_mul(h1, h3):
    return h1 * _sigmoid(h1) * h3


def _glu_body(a_ref, w1_ref, w3_ref, o_ref):
    a = a_ref[...]
    o_ref[...] = _silu_mul(_dot(a, w1_ref[...]), _dot(a, w3_ref[...])).astype(o_ref.dtype)


def _glu(a, w1, w3, *, tm=1024, tn=512, name="glu"):
    m, kdim = a.shape
    n = w1.shape[-1]
    tm = _tile(m, tm)
    tn = _tile(n, tn, LANE)
    assert m % tm == 0 and n % tn == 0
    return pl.pallas_call(
        _glu_body,
        grid=(m // tm, n // tn),
        in_specs=[pl.BlockSpec((tm, kdim), lambda i, j: (i, 0)),
                  pl.BlockSpec((kdim, tn), lambda i, j: (0, j)),
                  pl.BlockSpec((kdim, tn), lambda i, j: (0, j))],
        out_specs=pl.BlockSpec((tm, tn), lambda i, j: (i, j)),
        out_shape=jax.ShapeDtypeStruct((m, n), BF16),
        compiler_params=_params("parallel", "arbitrary"),
        name=name,
    )(a, w1, w3)


def _for_each_sub(n_live, live_fn, dead_fn):
    @pl.when(n_live == MOE_NSUB)
    def _():
        live_fn(slice(None))

    @pl.when(n_live < MOE_NSUB)
    def _():
        for s in range(MOE_NSUB):
            rows = slice(s * MOE_SUB, (s + 1) * MOE_SUB)
            pl.when(s < n_live)(functools.partial(live_fn, rows))
            pl.when(s >= n_live)(functools.partial(dead_fn, rows))


def _expert_glu_body(be_ref, ns_ref, a_ref, w1_ref, w3_ref, o_ref):
    del be_ref

    def live(rows):
        a = a_ref[rows, :]
        h1 = jnp.dot(a, w1_ref[...].astype(BF16), preferred_element_type=F32)
        h3 = jnp.dot(a, w3_ref[...].astype(BF16), preferred_element_type=F32)
        o_ref[rows, :] = _silu_mul(h1, h3).astype(o_ref.dtype)

    def dead(rows):
        o_ref[rows, :] = jnp.zeros((MOE_SUB, o_ref.shape[1]), o_ref.dtype)

    _for_each_sub(ns_ref[pl.program_id(0)], live, dead)


def _expert_glu(xs, w1, w3, layer, block_expert, n_sub, *, tn=512):
    m, kdim = xs.shape
    n = w1.shape[-1]
    tn = _tile(n, tn, LANE)
    nj = n // tn
    assert m % MOE_BLOCK == 0 and n % tn == 0

    def w_map(i, j, be, ns):
        return layer, be[i], 0, jnp.where(ns[i] > 0, j, nj - 1)

    gs = pltpu.PrefetchScalarGridSpec(
        num_scalar_prefetch=2,
        grid=(m // MOE_BLOCK, nj),
        in_specs=[pl.BlockSpec((MOE_BLOCK, kdim), lambda i, j, be, ns: (i, 0), pipeline_mode=pl.Buffered(1)),
                  pl.BlockSpec((None, None, kdim, tn), w_map),
                  pl.BlockSpec((None, None, kdim, tn), w_map)],
        out_specs=pl.BlockSpec((MOE_BLOCK, tn), lambda i, j, be, ns: (i, j)),
    )
    return pl.pallas_call(
        _expert_glu_body, grid_spec=gs, out_shape=jax.ShapeDtypeStruct((m, n), BF16),
        compiler_params=_params("parallel", "arbitrary"), name="expert_glu",
    )(block_expert, n_sub, xs, w1, w3)


def _expert_down_body(be_ref, ns_ref, a_ref, w_ref, o_ref, acc_ref, *, nk):
    del be_ref
    k = pl.program_id(2)

    def live(rows):
        part = jnp.dot(a_ref[rows, :], w_ref[...].astype(BF16), preferred_element_type=F32)

        @pl.when(k == 0)
        def _():
            acc_ref[rows, :] = part

        @pl.when(k > 0)
        def _():
            acc_ref[rows, :] += part

        @pl.when(k == nk - 1)
        def _():
            acc = acc_ref[rows, :]
            o_ref[rows] = acc.reshape((acc.shape[0],) + o_ref.shape[1:])

    def dead(rows):
        @pl.when(k == nk - 1)
        def _():
            o_ref[rows] = jnp.zeros((MOE_SUB,) + o_ref.shape[1:], o_ref.dtype)

    _for_each_sub(ns_ref[pl.program_id(0)], live, dead)


def _expert_down(hmid, w2, layer, block_expert, n_sub, *, tn=1024, tk=2048):
    m, kdim = hmid.shape
    n = w2.shape[-1]
    tn = _tile(n, tn, LANE)
    tk = _tile(kdim, tk, LANE)
    nj, nk = n // tn, kdim // tk
    assert m % MOE_BLOCK == 0 and n % tn == 0 and kdim % tk == 0

    def w_map(i, j, k, be, ns):
        live = ns[i] > 0
        return layer, be[i], jnp.where(live, k, nk - 1), jnp.where(live, j, nj - 1)

    gs = pltpu.PrefetchScalarGridSpec(
        num_scalar_prefetch=2,
        grid=(m // MOE_BLOCK, nj, nk),
        in_specs=[pl.BlockSpec((MOE_BLOCK, tk), lambda i, j, k, be, ns: (i, k)),
                  pl.BlockSpec((None, None, tk, tn), w_map)],
        out_specs=pl.BlockSpec((MOE_BLOCK, tn // LANE, LANE), lambda i, j, k, be, ns: (i, j, 0)),
        scratch_shapes=[pltpu.VMEM((MOE_BLOCK, tn), F32)],
    )
    return pl.pallas_call(
        functools.partial(_expert_down_body, nk=nk), grid_spec=gs,
        out_shape=jax.ShapeDtypeStruct((m, n // LANE, LANE), F32),
        compiler_params=_params("parallel", "parallel", "arbitrary"), name="expert_down",
    )(block_expert, n_sub, hmid, w2)


def _logf_body(a_ref, w_ref, b_ref, o_ref):
    z = _dot(a_ref[...], w_ref[...]) + b_ref[...]
    o_ref[...] = jnp.minimum(z, 0.0) - jnp.log1p(jnp.exp(-jnp.abs(z)))


def _logf(xn, w_f, b_f):
    t, d = xn.shape
    nh = w_f.shape[1]
    wpad = jnp.zeros((d, LANE), F32).at[:, :nh].set(w_f.astype(F32))
    bpad = jnp.zeros((1, LANE), F32).at[0, :nh].set(b_f.astype(F32))
    tr = _tile(t, 1024)
    out = pl.pallas_call(
        _logf_body,
        grid=(t // tr,),
        in_specs=[pl.BlockSpec((tr, d), lambda i: (i, 0)), pl.BlockSpec((d, LANE), lambda i: (0, 0)),
                  pl.BlockSpec((1, LANE), lambda i: (0, 0))],
        out_specs=pl.BlockSpec((tr, LANE), lambda i: (i, 0)),
        out_shape=jax.ShapeDtypeStruct((t, LANE), F32),
        compiler_params=_params("parallel"),
        name="logf",
    )(xn, wpad, bpad)
    return out[:, :nh]


def _cumsum_body(x_ref, o_ref, *, nchunk):
    r = lax.broadcasted_iota(jnp.int32, (LANE, LANE), 0)
    c = lax.broadcasted_iota(jnp.int32, (LANE, LANE), 1)
    tri = (r <= c).astype(BF16)
    rows = x_ref.shape[1]

    def step(i, carry):
        h, m, l = _split3(x_ref[i])
        cs = (jnp.dot(h, tri, preferred_element_type=F32) + jnp.dot(m, tri, preferred_element_type=F32)
              + jnp.dot(l, tri, preferred_element_type=F32)) + carry
        o_ref[i] = cs
        return cs[:, LANE - 1:LANE]

    lax.fori_loop(0, nchunk, step, jnp.zeros((rows, 1), F32))


def _cumsum_time(x):
    b, l, nh = x.shape
    lp = -(-l // LANE) * LANE
    nchunk = lp // LANE
    rows = b * nh
    rp = -(-rows // 8) * 8
    xt = jnp.transpose(x, (0, 2, 1)).reshape(rows, l)
    xt = jnp.pad(xt, ((0, rp - rows), (0, lp - l)))
    xc = jnp.transpose(xt.reshape(rp, nchunk, LANE), (1, 0, 2))
    out = pl.pallas_call(
        functools.partial(_cumsum_body, nchunk=nchunk),
        out_shape=jax.ShapeDtypeStruct((nchunk, rp, LANE), F32),
        compiler_params=pltpu.CompilerParams(vmem_limit_bytes=V7X_VMEM_LIMIT),
        name="cumsum_time",
    )(xc)
    out = jnp.transpose(out, (1, 0, 2)).reshape(rp, lp)[:rows, :l]
    return out.reshape(b, nh, l)


def _fox_update(q, k, v, cq, ck, mask, m_ref, l_ref, acc_ref, hd, dh, scale):
    s = _dot_nt(k, q) * scale
    s = s + cq - ck
    if mask is not None:
        s = jnp.where(mask, s, NEG_INF)
    m_prev = m_ref[hd]
    m_new = jnp.maximum(m_prev, jnp.max(s, axis=0, keepdims=True))
    alpha = jnp.exp(m_prev - m_new)
    p = jnp.exp(s - m_new)
    l_ref[hd] = alpha * l_ref[hd] + jnp.sum(p, axis=0, keepdims=True)
    rows = slice(hd * dh, (hd + 1) * dh)
    acc_ref[rows, :] = alpha * acc_ref[rows, :] + _dot_tn(v, p)
    m_ref[hd] = m_new


def _fox_init(m_ref, l_ref, acc_ref):
    m_ref[...] = jnp.full(m_ref.shape, NEG_INF, F32)
    l_ref[...] = jnp.zeros(l_ref.shape, F32)
    acc_ref[...] = jnp.zeros(acc_ref.shape, F32)


def _fox_finish(o_ref, m_ref, l_ref, acc_ref, nh, dh):
    for hd in range(nh):
        cols = slice(hd * dh, (hd + 1) * dh)
        o_ref[:, cols] = (acc_ref[cols, :] / l_ref[hd]).T.astype(o_ref.dtype)


def _fox_prompt_body(q_ref, k_ref, v_ref, cq_ref, ck_ref, o_ref, m_ref, l_ref, acc_ref, *, nh, dh, tq, nkv):
    qi = pl.program_id(1)
    kj = pl.program_id(2)

    @pl.when(kj == 0)
    def _():
        _fox_init(m_ref, l_ref, acc_ref)

    def update(mask):
        for hd in range(nh):
            cols = slice(hd * dh, (hd + 1) * dh)
            _fox_update(q_ref[:, cols], k_ref[:, cols], v_ref[:, cols], cq_ref[hd], ck_ref[:, hd:hd + 1], mask,
                        m_ref, l_ref, acc_ref, hd, dh, dh ** -0.5)

    @pl.when(kj < qi)
    def _():
        update(None)

    @pl.when(kj == qi)
    def _():
        key_pos = lax.broadcasted_iota(jnp.int32, (tq, tq), 0)
        update(key_pos <= lax.broadcasted_iota(jnp.int32, (tq, tq), 1))

    @pl.when(kj == nkv - 1)
    def _():
        _fox_finish(o_ref, m_ref, l_ref, acc_ref, nh, dh)


def _fox_prompt(h, csum, *, nb, seq, nh, dh, col_q, col_k, col_v):
    fw = nh * dh
    tq = _tile(seq, 512)
    nq = seq // tq
    assert col_q % fw == 0 and col_k % fw == 0 and col_v % fw == 0
    cq = csum.reshape(nb, nh, 1, seq)
    ck = jnp.transpose(csum, (0, 2, 1))
    body = functools.partial(_fox_prompt_body, nh=nh, dh=dh, tq=tq, nkv=nq)
    return pl.pallas_call(
        body,
        grid=(nb, nq, nq),
        in_specs=[
            pl.BlockSpec((tq, fw), lambda b, i, j: (b * nq + i, col_q // fw)),
            pl.BlockSpec((tq, fw), lambda b, i, j: (b * nq + jnp.minimum(i, j), col_k // fw)),
            pl.BlockSpec((tq, fw), lambda b, i, j: (b * nq + jnp.minimum(i, j), col_v // fw)),
            pl.BlockSpec((None, nh, 1, tq), lambda b, i, j: (b, 0, 0, i)),
            pl.BlockSpec((None, tq, nh), lambda b, i, j: (b, jnp.minimum(i, j), 0)),
        ],
        out_specs=pl.BlockSpec((tq, fw), lambda b, i, j: (b * nq + i, 0)),
        out_shape=jax.ShapeDtypeStruct((nb * seq, fw), BF16),
        scratch_shapes=[pltpu.VMEM((nh, 1, tq), F32), pltpu.VMEM((nh, 1, tq), F32), pltpu.VMEM((fw, tq), F32)],
        compiler_params=_params("parallel", "parallel", "arbitrary"),
        name="fox_prompt",
    )(h, h, h, cq, ck)


def _fox_sample_body(q_ref, kn_ref, vn_ref, kp_ref, vp_ref, cq_ref, ckp_ref, ckn_ref, o_ref,
                     m_ref, l_ref, acc_ref, *, nh, dh, lq, npast, tkp):
    kj = pl.program_id(1)
    scale = dh ** -0.5

    @pl.when(kj == 0)
    def _():
        _fox_init(m_ref, l_ref, acc_ref)

    @pl.when(kj < npast)
    def _():
        for hd in range(nh):
            cols = slice(hd * dh, (hd + 1) * dh)
            head_rows = pl.ds(hd, tkp, stride=nh)
            _fox_update(q_ref[:, cols], kp_ref[head_rows, :], vp_ref[head_rows, :], cq_ref[hd],
                        ckp_ref[:, hd:hd + 1], None, m_ref, l_ref, acc_ref, hd, dh, scale)

    @pl.when(kj == npast)
    def _():
        key_pos = lax.broadcasted_iota(jnp.int32, (lq, lq), 0)
        mask = key_pos <= lax.broadcasted_iota(jnp.int32, (lq, lq), 1)
        for hd in range(nh):
            cols = slice(hd * dh, (hd + 1) * dh)
            _fox_update(q_ref[:, cols], kn_ref[:, cols], vn_ref[:, cols], cq_ref[hd], ckn_ref[:, hd:hd + 1], mask,
                        m_ref, l_ref, acc_ref, hd, dh, scale)
        _fox_finish(o_ref, m_ref, l_ref, acc_ref, nh, dh)


def _fox_sample(h, past_k, past_v, layer, csum, *, row0, nb, lq, nh, dh, col_q, col_k, col_v):
    fw = nh * dh
    depth, _, plen = past_k.shape[:3]
    tkp = _tile(plen, 1024)
    npast = plen // tkp
    assert row0 % lq == 0 and col_q % fw == 0 and col_k % fw == 0 and col_v % fw == 0
    rb = row0 // lq
    cq = csum[:, :, plen:].reshape(nb, nh, 1, lq)
    ck = jnp.transpose(csum, (0, 2, 1))
    ckp = ck[:, :plen]
    ckn = ck[:, plen:]
    past_k = past_k.reshape(depth, nb, plen * nh, dh)
    past_v = past_v.reshape(depth, nb, plen * nh, dh)
    body = functools.partial(_fox_sample_body, nh=nh, dh=dh, lq=lq, npast=npast, tkp=tkp)
    pmap = lambda b, j: (layer, b, jnp.minimum(j, npast - 1), 0)
    return pl.pallas_call(
        body,
        grid=(nb, npast + 1),
        in_specs=[
            pl.BlockSpec((lq, fw), lambda b, j: (rb + b, col_q // fw)),
            pl.BlockSpec((lq, fw), lambda b, j: (rb + b, col_k // fw)),
            pl.BlockSpec((lq, fw), lambda b, j: (rb + b, col_v // fw)),
            pl.BlockSpec((None, None, tkp * nh, dh), pmap),
            pl.BlockSpec((None, None, tkp * nh, dh), pmap),
            pl.BlockSpec((None, nh, 1, lq), lambda b, j: (b, 0, 0, 0)),
            pl.BlockSpec((None, tkp, nh), lambda b, j: (b, jnp.minimum(j, npast - 1), 0)),
            pl.BlockSpec((None, lq, nh), lambda b, j: (b, 0, 0)),
        ],
        out_specs=pl.BlockSpec((lq, fw), lambda b, j: (b, 0)),
        out_shape=jax.ShapeDtypeStruct((nb * lq, fw), BF16),
        scratch_shapes=[pltpu.VMEM((nh, 1, lq), F32), pltpu.VMEM((nh, 1, lq), F32), pltpu.VMEM((fw, lq), F32)],
        compiler_params=_params("parallel", "arbitrary"),
        name="fox_sample",
    )(h, h, h, past_k, past_v, cq, ckp, ckn)


def _pool_body(u_ref, prev_ref, hist_ref, w_ref, sc_ref, o_ref, ext_ref, *, tl, gd, start_pos):
    i = pl.program_id(1)
    first = i == 0
    ext_ref[0:HALO, :] = jnp.where(first, hist_ref[...], prev_ref[...])
    ext_ref[HALO:HALO + tl, :] = u_ref[...]
    pos = start_pos + i * tl + lax.broadcasted_iota(jnp.int32, (tl, 1), 0)
    for g, w in enumerate(POOL_WINDOWS):
        cols = slice(g * gd, (g + 1) * gd)
        cur = ext_ref[HALO:HALO + tl, cols]
        win = cur
        for d in range(1, w):
            win = win + ext_ref[HALO - d:HALO - d + tl, cols]
        cnt = jnp.minimum(pos + 1, w).astype(F32)
        pooled = win / cnt - cur
        y = _dot(pooled, w_ref[g]) * sc_ref[:, cols]
        o_ref[:, cols] = y.astype(o_ref.dtype)


def _pool(h, hist, w_pool, pool_scale, *, row0, nb, seq, col_u, start_pos):
    ng, gd, _ = w_pool.shape
    c = ng * gd
    nhist = hist.shape[1]
    assert nhist < HALO and max(POOL_WINDOWS) - 1 <= nhist and ng == len(POOL_WINDOWS)
    tl = _tile(seq, 512, HALO)
    nl = seq // tl
    assert col_u % c == 0 and row0 % tl == 0 and tl % HALO == 0 and seq % HALO == 0
    rb = row0 // tl
    hb = tl // HALO
    hist_p = jnp.concatenate([jnp.zeros((nb, HALO - nhist, c), F32), hist.astype(F32)], axis=1)
    body = functools.partial(_pool_body, tl=tl, gd=gd, start_pos=start_pos)
    return pl.pallas_call(
        body,
        grid=(nb, nl),
        in_specs=[
            pl.BlockSpec((tl, c), lambda b, i: (rb + b * nl + i, col_u // c)),
            pl.BlockSpec((HALO, c), lambda b, i: (jnp.maximum((rb + b * nl + i) * hb - 1, 0), col_u // c)),
            pl.BlockSpec((None, HALO, c), lambda b, i: (b, 0, 0)),
            pl.BlockSpec((ng, gd, gd), lambda b, i: (0, 0, 0)),
            pl.BlockSpec((1, c), lambda b, i: (0, 0)),
        ],
        out_specs=pl.BlockSpec((tl, c), lambda b, i: (b * nl + i, 0)),
        out_shape=jax.ShapeDtypeStruct((nb * seq, c), BF16),
        scratch_shapes=[pltpu.VMEM((HALO + tl, c), F32)],
        compiler_params=_params("parallel", "arbitrary"),
        name="pool",
    )(h, h, hist_p, w_pool.astype(BF16), pool_scale.reshape(1, c).astype(F32))


def _ret_body(q_ref, k_ref, v_ref, g_ref, cos_ref, sin_ref, ld_ref, s0_ref, o_ref, sout_ref, s_ref, *, ch, dk, nchunk):
    n = pl.program_id(2)

    @pl.when(n == 0)
    def _():
        s_ref[...] = s0_ref[...]

    ld = ld_ref[:, 0:1]
    cos_t = cos_ref[...]
    sin_t = sin_ref[...]

    def rope(x):
        return x * cos_t + pltpu.roll(x, dk // 2, axis=1) * sin_t

    q = rope(q_ref[...])
    k = rope(k_ref[...]) * (dk ** -0.5)
    v = v_ref[...]
    ii = lax.broadcasted_iota(jnp.int32, (ch, 1), 0).astype(F32)
    diff = ii - lax.broadcasted_iota(jnp.int32, (1, ch), 1).astype(F32)
    dmask = jnp.where(diff >= 0, jnp.exp(ld * jnp.maximum(diff, 0.0)), 0.0)
    scores = _dot_nt(q, k) * dmask
    o = _dot(scores, v)
    s_prev = s_ref[...]
    o = o + _dot(q * jnp.exp(ld * (ii + 1.0)), s_prev)
    k_dec = k * jnp.exp(ld * (ch - 1.0 - ii))
    s_ref[...] = jnp.exp(ld * ch) * s_prev + _dot_tn(k_dec, v)
    o = o * lax.rsqrt(jnp.mean(o * o, axis=-1, keepdims=True) + EPS)
    gate = g_ref[...]
    o_ref[...] = (o * (gate * _sigmoid(gate))).astype(o_ref.dtype)

    @pl.when(n == nchunk - 1)
    def _():
        sout_ref[...] = s_ref[...]


def _retention(h, state0, *, row0, nb, seq, col_q, col_k, col_v, col_g, start_pos):
    _, nh, dk, dv = state0.shape
    ch = _tile(seq, 256)
    nchunk = seq // ch
    assert row0 % ch == 0 and col_q % dk == 0 and col_k % dk == 0 and col_v % dv == 0 and col_g % dv == 0
    rb = row0 // ch
    half = dk // 2
    inv = ROPE_BASE ** (-jnp.arange(half, dtype=F32) / half)
    ang = (start_pos + jnp.arange(seq, dtype=jnp.int32)).astype(F32)[:, None] * inv[None, :]
    cos_t = jnp.concatenate([jnp.cos(ang), jnp.cos(ang)], axis=1)
    sin_t = jnp.concatenate([-jnp.sin(ang), jnp.sin(ang)], axis=1)
    ld = jnp.log1p(-jnp.exp2(-5.0 - jnp.arange(nh, dtype=F32)))
    ldv = jnp.broadcast_to(ld[:, None, None], (nh, 1, LANE))
    body = functools.partial(_ret_body, ch=ch, dk=dk, nchunk=nchunk)
    rows = lambda b, hd, n: rb + b * nchunk + n
    return pl.pallas_call(
        body,
        grid=(nb, nh, nchunk),
        in_specs=[
            pl.BlockSpec((ch, dk), lambda b, hd, n: (rows(b, hd, n), col_q // dk + hd)),
            pl.BlockSpec((ch, dk), lambda b, hd, n: (rows(b, hd, n), col_k // dk + hd)),
            pl.BlockSpec((ch, dv), lambda b, hd, n: (rows(b, hd, n), col_v // dv + hd)),
            pl.BlockSpec((ch, dv), lambda b, hd, n: (rows(b, hd, n), col_g // dv + hd)),
            pl.BlockSpec((ch, dk), lambda b, hd, n: (n, 0)),
            pl.BlockSpec((ch, dk), lambda b, hd, n: (n, 0)),
            pl.BlockSpec((None, 1, LANE), lambda b, hd, n: (hd, 0, 0)),
            pl.BlockSpec((None, None, dk, dv), lambda b, hd, n: (b, hd, 0, 0)),
        ],
        out_specs=[
            pl.BlockSpec((ch, dv), lambda b, hd, n: (b * nchunk + n, hd)),
            pl.BlockSpec((None, None, dk, dv), lambda b, hd, n: (b, hd, 0, 0)),
        ],
        out_shape=[jax.ShapeDtypeStruct((nb * seq, nh * dv), BF16),
                   jax.ShapeDtypeStruct((nb, nh, dk, dv), F32)],
        scratch_shapes=[pltpu.VMEM((dk, dv), F32)],
        compiler_params=_params("parallel", "parallel", "arbitrary"),
        name="retention",
    )(h, h, h, h, cos_t, sin_t, ldv, state0.astype(F32))


def _merge_body(fo_ref, po_ref, ro_ref, wf_ref, wp_ref, wr_ref, g0_ref, g1_ref, g2_ref, o_ref):
    acc = _sigmoid(g0_ref[...]) * _dot(fo_ref[...], wf_ref[...])
    acc = acc + _sigmoid(g1_ref[...]) * _dot(po_ref[...], wp_ref[...])
    acc = acc + _sigmoid(g2_ref[...]) * _dot(ro_ref[...], wr_ref[...])
    o_ref[...] = acc.astype(o_ref.dtype)


def _merge(fo, po, ro, wf, wp, wr, h, *, col_g, d):
    t = fo.shape[0]
    tm = _tile(t, 1024)
    tn = _tile(math.gcd(d, col_g), 512, LANE)
    gb = col_g // tn
    nd = d // tn
    kf, kp, kr = fo.shape[1], po.shape[1], ro.shape[1]
    return pl.pallas_call(
        _merge_body,
        grid=(t // tm, nd),
        in_specs=[
            pl.BlockSpec((tm, kf), lambda i, j: (i, 0)),
            pl.BlockSpec((tm, kp), lambda i, j: (i, 0)),
            pl.BlockSpec((tm, kr), lambda i, j: (i, 0)),
            pl.BlockSpec((kf, tn), lambda i, j: (0, j)),
            pl.BlockSpec((kp, tn), lambda i, j: (0, j)),
            pl.BlockSpec((kr, tn), lambda i, j: (0, j)),
            pl.BlockSpec((tm, tn), lambda i, j: (i, gb + j)),
            pl.BlockSpec((tm, tn), lambda i, j: (i, gb + nd + j)),
            pl.BlockSpec((tm, tn), lambda i, j: (i, gb + 2 * nd + j)),
        ],
        out_specs=pl.BlockSpec((tm, tn), lambda i, j: (i, j)),
        out_shape=jax.ShapeDtypeStruct((t, d), BF16),
        compiler_params=_params("parallel", "arbitrary"),
        name="merge",
    )(fo, po, ro, wf, wp, wr, h, h, h)


def _xattn_body(q_ref, k_ref, v_ref, o_ref, *, nh, dh):
    scale = dh ** -0.5
    for hd in range(nh):
        cols = slice(hd * dh, (hd + 1) * dh)
        s = _dot_nt(q_ref[:, cols], k_ref[:, cols]) * scale
        p = jnp.exp(s - jnp.max(s, axis=-1, keepdims=True))
        o = _dot(p, v_ref[:, cols]) / jnp.sum(p, axis=-1, keepdims=True)
        o_ref[:, cols] = o.astype(o_ref.dtype)


def _xattn(q, mk, mv, *, row0, nb, seq, nh, dh):
    w = nh * dh
    mt = mk.shape[1]
    tq = _tile(seq, 512)
    nq = seq // tq
    assert row0 % tq == 0
    rb = row0 // tq
    return pl.pallas_call(
        functools.partial(_xattn_body, nh=nh, dh=dh),
        grid=(nb, nq),
        in_specs=[
            pl.BlockSpec((tq, w), lambda b, i: (rb + b * nq + i, 0)),
            pl.BlockSpec((None, mt, w), lambda b, i: (b, 0, 0)),
            pl.BlockSpec((None, mt, w), lambda b, i: (b, 0, 0)),
        ],
        out_specs=pl.BlockSpec((tq, w), lambda b, i: (b * nq + i, 0)),
        out_shape=jax.ShapeDtypeStruct((nb * seq, w), BF16),
        compiler_params=_params("parallel", "arbitrary"),
        name="xattn",
    )(q, mk, mv)


def _rank_body(sel_ref, rk_ref, cnt_ref, carry_ref, *, tb):
    i = pl.program_id(0)

    @pl.when(i == 0)
    def _():
        carry_ref[...] = jnp.zeros_like(carry_ref)

    sel = sel_ref[...]
    lane = lax.broadcasted_iota(jnp.int32, sel.shape, 1)
    lanef = lane.astype(F32)
    e1 = sel[:, 0:1]
    e2 = sel[:, 1:2]
    onehot = jnp.logical_or(lanef == e1, lanef == e2).astype(F32)
    r = lax.broadcasted_iota(jnp.int32, (tb, tb), 0)
    c = lax.broadcasted_iota(jnp.int32, (tb, tb), 1)
    before = (c < r).astype(BF16)
    rank = jnp.dot(before, onehot.astype(BF16), preferred_element_type=F32) + carry_ref[...]
    r1 = jnp.sum(jnp.where(lanef == e1, rank, 0.0), axis=-1, keepdims=True)
    r2 = jnp.sum(jnp.where(lanef == e2, rank, 0.0), axis=-1, keepdims=True)
    rk_ref[...] = jnp.where(lane == 0, r1, jnp.where(lane == 1, r2, 0.0))
    total = carry_ref[...] + jnp.sum(onehot, axis=0, keepdims=True)
    carry_ref[...] = total
    cnt_ref[...] = total


def _expert_ranks(sel):
    t = sel.shape[0]
    tb = _tile(t, 512)
    return pl.pallas_call(
        functools.partial(_rank_body, tb=tb),
        grid=(t // tb,),
        in_specs=[pl.BlockSpec((tb, LANE), lambda i: (i, 0))],
        out_specs=[pl.BlockSpec((tb, LANE), lambda i: (i, 0)), pl.BlockSpec((1, LANE), lambda i: (0, 0))],
        out_shape=[jax.ShapeDtypeStruct((t, LANE), F32), jax.ShapeDtypeStruct((1, LANE), F32)],
        scratch_shapes=[pltpu.VMEM((1, LANE), F32)],
        compiler_params=_params("arbitrary"),
        name="expert_ranks",
    )(sel)


def _row_copy(src, dst, sem, s, d):
    return pltpu.make_async_copy(src.at[s], dst.at[d], sem)


def _dispatch_body(ns_ref, tok_ref, x_ref, o_ref, buf_ref, sem):
    n_live = ns_ref[pl.program_id(0)]
    n_rows = n_live * MOE_SUB

    def issue(r, c):
        _row_copy(x_ref, buf_ref, sem, tok_ref[0, 0, r], r).start()
        return c

    lax.fori_loop(0, n_rows, issue, 0)

    def drain(r, c):
        _row_copy(x_ref, buf_ref, sem, 0, 0).wait()
        return c

    lax.fori_loop(0, n_rows, drain, 0)
    d = o_ref.shape[1]
    for s in range(MOE_NSUB):
        rows = slice(s * MOE_SUB, (s + 1) * MOE_SUB)

        @pl.when(s < n_live)
        def _():
            o_ref[rows, :] = buf_ref[rows].reshape(MOE_SUB, d)

        @pl.when(s >= n_live)
        def _():
            o_ref[rows, :] = jnp.zeros((MOE_SUB, d), o_ref.dtype)


def _dispatch(x3, slot_tok, n_sub):
    t, s, _ = x3.shape
    n_slots = slot_tok.shape[0]
    n_blocks = n_slots // MOE_BLOCK
    gs = pltpu.PrefetchScalarGridSpec(
        num_scalar_prefetch=1,
        grid=(n_blocks,),
        in_specs=[pl.BlockSpec((1, 1, MOE_BLOCK), lambda i, ns: (i, 0, 0), memory_space=pltpu.SMEM),
                  pl.BlockSpec(memory_space=pl.ANY)],
        out_specs=pl.BlockSpec((MOE_BLOCK, s * LANE), lambda i, ns: (i, 0)),
        scratch_shapes=[pltpu.VMEM((MOE_BLOCK, s, LANE), x3.dtype), pltpu.SemaphoreType.DMA(())],
    )
    return pl.pallas_call(
        _dispatch_body, grid_spec=gs, out_shape=jax.ShapeDtypeStruct((n_slots, s * LANE), x3.dtype),
        compiler_params=_params("arbitrary"), name="moe_dispatch",
    )(n_sub, slot_tok.reshape(n_blocks, 1, MOE_BLOCK), x3)


def _combine_body(dest_ref, g0_ref, g1_ref, x_ref, y_ref, o_ref, r0_ref, r1_ref, sem, *, tb):
    bufs = (r0_ref, r1_ref)

    def issue(r, c):
        for k in range(TOP_K):
            _row_copy(y_ref, bufs[k], sem, dest_ref[0, 0, TOP_K * r + k], r).start()
        return c

    lax.fori_loop(0, tb, issue, 0)

    def drain(r, c):
        _row_copy(y_ref, r0_ref, sem, 0, 0).wait()
        return c

    lax.fori_loop(0, TOP_K * tb, drain, 0)
    shape = x_ref.shape
    y = r0_ref[...].reshape(shape) * g0_ref[...] + r1_ref[...].reshape(shape) * g1_ref[...]
    o_ref[...] = x_ref[...] + y


def _combine(x, yb3, dest, sel):
    t, d = x.shape
    tb = _tile(t, 256)
    s = d // LANE
    dest3 = dest.reshape(t // tb, 1, TOP_K * tb)
    row = pl.BlockSpec((tb, d), lambda i: (i, 0))
    gate = pl.BlockSpec((tb, 1), lambda i: (i, 0))
    return pl.pallas_call(
        functools.partial(_combine_body, tb=tb),
        grid=(t // tb,),
        in_specs=[pl.BlockSpec((1, 1, TOP_K * tb), lambda i: (i, 0, 0), memory_space=pltpu.SMEM),
                  gate, gate, row, pl.BlockSpec(memory_space=pl.ANY)],
        out_specs=row,
        out_shape=jax.ShapeDtypeStruct((t, d), F32),
        scratch_shapes=[pltpu.VMEM((tb, s, LANE), F32), pltpu.VMEM((tb, s, LANE), F32),
                        pltpu.SemaphoreType.DMA(())],
        compiler_params=_params("arbitrary"),
        name="moe_combine",
    )(dest3, sel[:, 2:3], sel[:, 3:4], x, yb3)


def _moe_ffn(x, g, router, w1, w3, w2, layer):
    t, d = x.shape
    n_exp = router.shape[1]
    xn3, sel = _rmsnorm_router(x, g, router)
    rk, cnt = _expert_ranks(sel)
    counts = cnt[0, :n_exp].astype(jnp.int32)
    region = ((counts + MOE_BLOCK - 1) // MOE_BLOCK) * MOE_BLOCK
    region_end = jnp.cumsum(region)
    region_start = region_end - region
    n_blocks = -(-(t * TOP_K) // MOE_BLOCK) + n_exp
    n_slots = n_blocks * MOE_BLOCK
    eid = sel[:, :TOP_K].astype(jnp.int32)
    dest = (region_start[eid] + rk[:, :TOP_K].astype(jnp.int32)).astype(jnp.int32)
    block_start = jnp.arange(n_blocks, dtype=jnp.int32) * MOE_BLOCK
    block_exp = jnp.minimum(jnp.sum(block_start[:, None] >= region_end[None, :], axis=1), n_exp - 1).astype(jnp.int32)
    used = block_start < region_end[-1]
    rows_left = counts[block_exp] - (block_start - region_start[block_exp])
    n_sub = jnp.where(used, jnp.clip((rows_left + MOE_SUB - 1) // MOE_SUB, 0, MOE_NSUB), 0).astype(jnp.int32)
    last_used = jnp.maximum(region_end[-1] // MOE_BLOCK - 1, 0)
    block_exp = jnp.where(used, block_exp, block_exp[last_used]).astype(jnp.int32)
    tok = jnp.repeat(jnp.arange(t, dtype=jnp.int32), TOP_K)
    slot_tok = jnp.zeros((n_slots,), jnp.int32).at[dest.reshape(-1)].set(tok)
    xs = _dispatch(xn3, slot_tok, n_sub)
    hmid = _expert_glu(xs, w1, w3, layer, block_exp, n_sub)
    yb3 = _expert_down(hmid, w2, layer, block_exp, n_sub)
    return _combine(x, yb3, dest, sel)


def _pad_cols(w, mult):
    n = w.shape[-1]
    return jnp.pad(w, [(0, 0)] * (w.ndim - 1) + [(0, -(-n // mult) * mult - n)])


def _pad_rows(w, mult):
    n = w.shape[-2]
    return jnp.pad(w, [(0, 0)] * (w.ndim - 2) + [(0, -(-n // mult) * mult - n), (0, 0)])


def kernel(x_prompt, x_sample, mem_prompt, cache_fox_k, cache_fox_v, cache_fox_logf, state_pool, state_ret, cache_mem_k, cache_mem_v, norm_mix_g, w_in, b_forget, w_pool, pool_scale, w_br_fox, w_br_pool, w_br_ret, w_out, norm_x_g, norm_mem_g, w_xq, w_xk, w_xv, w_xo, norm_ffn_g, ffn_w1, ffn_w3, ffn_w2, moe_router, moe_w1, moe_w3, moe_w2, norm_final_g):
    bp, sp, d = x_prompt.shape
    bs, ss, _ = x_sample.shape
    depth = w_in.shape[0]
    past = cache_fox_k.shape[2]
    nh_f, dh_f = cache_fox_k.shape[3], cache_fox_k.shape[4]
    fw = nh_f * dh_f
    nhist, pw = state_pool.shape[2], state_pool.shape[3]
    nh_r, dk_r, dv_r = state_ret.shape[2], state_ret.shape[3], state_ret.shape[4]
    rqk, rvw = nh_r * dk_r, nh_r * dv_r
    mt, nh_m, dh_m = cache_mem_k.shape[2], cache_mem_k.shape[3], cache_mem_k.shape[4]
    mw = nh_m * dh_m
    tp, ts = bp * sp, bs * ss
    assert sp >= nhist and ss >= nhist

    c_fq, c_fk, c_fv = 0, fw, 2 * fw
    c_pu = 0
    c_rq = c_pu + pw
    c_rk = c_rq + rqk
    c_rv = c_rk + rqk
    c_rg = c_rv + rvw
    c_g = c_rg + rvw
    in_width = w_in.shape[2]
    assert in_width == 3 * fw + nh_f + c_g + 3 * d and in_width % 8 == 0 and (3 * fw + nh_f) % 8 == 0
    w_in_t = jnp.swapaxes(w_in, 1, 2).reshape(depth * in_width, d)

    x = jnp.concatenate([x_prompt.reshape(tp, d), x_sample.reshape(ts, d)], axis=0).astype(F32)
    outs = {k: [] for k in ("pk", "pv", "plf", "pp", "pr", "pmk", "pmv", "sk", "sv", "slf", "spool", "sr")}

    for l in range(depth):
        w_flog = w_in_t[l * in_width + 3 * fw:l * in_width + 3 * fw + nh_f].T

        xn = _rmsnorm(x, norm_mix_g[l], BF16)
        ha = _matmul_wst(xn, w_in_t, row0=l * in_width, n=3 * fw, out_dtype=F32, name="proj_in_fox")
        hb = _matmul_wst(xn, w_in_t, row0=l * in_width + 3 * fw + nh_f, n=c_g + 3 * d, out_dtype=F32, name="proj_in")
        logf = _logf(xn, w_flog, b_forget[l])
        logf_p = logf[:tp].reshape(bp, sp, nh_f)
        logf_s = logf[tp:].reshape(bs, ss, nh_f)
        cs_p = _cumsum_time(logf_p)
        cs_s = _cumsum_time(jnp.concatenate([cache_fox_logf[l].astype(F32), logf_s], axis=1))
        fo_p = _fox_prompt(ha, cs_p, nb=bp, seq=sp, nh=nh_f, dh=dh_f, col_q=c_fq, col_k=c_fk, col_v=c_fv)
        fo_s = _fox_sample(ha, cache_fox_k, cache_fox_v, l, cs_s,
                           row0=tp, nb=bs, lq=ss, nh=nh_f, dh=dh_f, col_q=c_fq, col_k=c_fk, col_v=c_fv)
        fo = jnp.concatenate([fo_p, fo_s], axis=0)

        po_p = _pool(hb, jnp.zeros((bp, nhist, pw), F32), w_pool[l], pool_scale[l],
                     row0=0, nb=bp, seq=sp, col_u=c_pu, start_pos=0)
        po_s = _pool(hb, state_pool[l], w_pool[l], pool_scale[l],
                     row0=tp, nb=bs, seq=ss, col_u=c_pu, start_pos=past)
        po = jnp.concatenate([po_p, po_s], axis=0)

        ro_p, rs_p = _retention(hb, jnp.zeros((bp, nh_r, dk_r, dv_r), F32), row0=0, nb=bp, seq=sp,
                                col_q=c_rq, col_k=c_rk, col_v=c_rv, col_g=c_rg, start_pos=0)
        ro_s, rs_s = _retention(hb, state_ret[l], row0=tp, nb=bs, seq=ss,
                                col_q=c_rq, col_k=c_rk, col_v=c_rv, col_g=c_rg, start_pos=past)
        ro = jnp.concatenate([ro_p, ro_s], axis=0)

        merged = _merge(fo, po, ro, w_br_fox[l].astype(BF16), w_br_pool[l].astype(BF16),
                        w_br_ret[l].astype(BF16), hb, col_g=c_g, d=d)
        x = _matmul_ws(merged, w_out, layer=l, out_dtype=F32, res=x, name="proj_out")

        mn = _rmsnorm(mem_prompt.reshape(bp * mt, d).astype(F32), norm_mem_g[l], BF16)
        mk = _matmul_ws(mn, w_xk, layer=l, out_dtype=F32, name="mem_k")
        mv = _matmul_ws(mn, w_xv, layer=l, out_dtype=F32, name="mem_v")
        xn = _rmsnorm(x, norm_x_g[l], BF16)
        q = _matmul_ws(xn, w_xq, layer=l, out_dtype=BF16, name="xattn_q")
        xo_p = _xattn(q, mk.reshape(bp, mt, mw), mv.reshape(bp, mt, mw), row0=0, nb=bp, seq=sp, nh=nh_m, dh=dh_m)
        xo_s = _xattn(q, cache_mem_k[l].reshape(bs, mt, mw).astype(F32), cache_mem_v[l].reshape(bs, mt, mw).astype(F32),
                      row0=tp, nb=bs, seq=ss, nh=nh_m, dh=dh_m)
        x = _matmul_ws(jnp.concatenate([xo_p, xo_s], axis=0), w_xo, layer=l, out_dtype=F32, res=x,
                       name="xattn_out")

        j = l // 2
        if l % 2 == 0:
            xn = _rmsnorm(x, norm_ffn_g[l], BF16)
            w1 = _pad_cols(ffn_w1[j], 1024).astype(BF16)
            w3 = _pad_cols(ffn_w3[j], 1024).astype(BF16)
            w2 = _pad_rows(ffn_w2[j], 1024).astype(BF16)
            hmid = _glu(xn, w1, w3, name="ffn_glu")
            x = _matmul(hmid, w2, out_dtype=F32, res=x, tk=w2.shape[0] // 4, name="ffn_down")
        else:
            x = _moe_ffn(x, norm_ffn_g[l], moe_router[j], moe_w1, moe_w3, moe_w2, j)

        outs["pk"].append(ha[:tp, c_fk:c_fk + fw].reshape(bp, sp, nh_f, dh_f))
        outs["pv"].append(ha[:tp, c_fv:c_fv + fw].reshape(bp, sp, nh_f, dh_f))
        outs["plf"].append(logf_p)
        outs["pp"].append(hb[:tp, c_pu:c_pu + pw].reshape(bp, sp, pw)[:, sp - nhist:])
        outs["pr"].append(rs_p)
        outs["pmk"].append(mk.reshape(bp, mt, nh_m, dh_m))
        outs["pmv"].append(mv.reshape(bp, mt, nh_m, dh_m))
        outs["sk"].append(ha[tp:, c_fk:c_fk + fw].reshape(bs, ss, nh_f, dh_f))
        outs["sv"].append(ha[tp:, c_fv:c_fv + fw].reshape(bs, ss, nh_f, dh_f))
        outs["slf"].append(logf_s)
        outs["spool"].append(hb[tp:, c_pu:c_pu + pw].reshape(bs, ss, pw)[:, ss - nhist:])
        outs["sr"].append(rs_s)

    y = _rmsnorm(x, norm_final_g, F32)
    st = {k: jnp.stack(v) for k, v in outs.items()}
    return (y[:tp].reshape(bp, sp, d), y[tp:].reshape(bs, ss, d),
            st["pk"], st["pv"], st["plf"], st["pp"], st["pr"], st["pmk"], st["pmv"],
            st["sk"], st["sv"], st["slf"], st["spool"], st["sr"])
```

```python
import functools
import math

import jax
import jax.numpy as jnp
from jax import lax
from jax.experimental import pallas as pl
from jax.experimental.pallas import tpu as pltpu

EPS = 1e-6
NEG_INF = -1e30
POOL_WINDOWS = (2, 4, 8, 16)
ROPE_BASE = 10000.0
TOP_K = 2

LANE = 128
HALO = 16
V7X_VMEM_LIMIT = 56 * 1024 * 1024
MOE_BLOCK = 1024
MOE_SUB = 256
MOE_NSUB = MOE_BLOCK // MOE_SUB

F32 = jnp.float32
BF16 = jnp.bfloat16


def _tile(n, pref, mult=8):
    if n <= pref:
        return n
    t = (pref // mult) * mult
    while t >= mult:
        if n % t == 0:
            return t
        t -= mult
    return n


def _params(*sem):
    return pltpu.CompilerParams(dimension_semantics=sem, vmem_limit_bytes=V7X_VMEM_LIMIT)


def _dot(a, b):
    return jnp.dot(a.astype(BF16), b.astype(BF16), preferred_element_type=F32)


def _dot_nt(a, b):
    return lax.dot_general(a.astype(BF16), b.astype(BF16), (((1,), (1,)), ((), ())), preferred_element_type=F32)


def _dot_tn(a, b):
    return lax.dot_general(a.astype(BF16), b.astype(BF16), (((0,), (0,)), ((), ())), preferred_element_type=F32)


def _sigmoid(x):
    return 1.0 / (1.0 + jnp.exp(-x))


def _rmsnorm_body(x_ref, g_ref, o_ref):
    x = x_ref[...]
    y = x * lax.rsqrt(jnp.mean(x * x, axis=-1, keepdims=True) + EPS)
    o_ref[...] = (y * g_ref[...]).astype(o_ref.dtype)


def _rmsnorm(x, g, out_dtype, row0=0, rows=None):
    d = x.shape[1]
    t = x.shape[0] if rows is None else rows
    tr = _tile(math.gcd(t, row0) if row0 else t, 256)
    rb = row0 // tr
    return pl.pallas_call(
        _rmsnorm_body,
        grid=(t // tr,),
        in_specs=[pl.BlockSpec((tr, d), lambda i: (rb + i, 0)), pl.BlockSpec((1, d), lambda i: (0, 0))],
        out_specs=pl.BlockSpec((tr, d), lambda i: (i, 0)),
        out_shape=jax.ShapeDtypeStruct((t, d), out_dtype),
        compiler_params=_params("parallel"),
        name="rmsnorm",
    )(x, g.reshape(1, d).astype(F32))


def _split3(x):
    h = x.astype(BF16)
    r = x - h.astype(F32)
    m = r.astype(BF16)
    l = (r - m.astype(F32)).astype(BF16)
    return h, m, l


def _rmsnorm_router_body(x_ref, g_ref, w_ref, o_ref, sel_ref, *, n_exp):
    x = x_ref[...]
    y = x * lax.rsqrt(jnp.mean(x * x, axis=-1, keepdims=True) + EPS) * g_ref[...]
    o_ref[...] = y.reshape(o_ref.shape).astype(o_ref.dtype)
    yh, ym, _ = _split3(y)
    w = w_ref[...]
    wh, wm, _ = _split3(w)
    logits = (jnp.dot(yh, wh, preferred_element_type=F32) + jnp.dot(yh, wm, preferred_element_type=F32)
              + jnp.dot(ym, wh, preferred_element_type=F32))
    lane = lax.broadcasted_iota(jnp.int32, logits.shape, 1)
    logits = jnp.where(lane < n_exp, logits, NEG_INF)
    m1 = jnp.max(logits, axis=-1, keepdims=True)
    i1 = jnp.min(jnp.where(logits == m1, lane, LANE), axis=-1, keepdims=True)
    rest = jnp.where(lane == i1, NEG_INF, logits)
    m2 = jnp.max(rest, axis=-1, keepdims=True)
    i2 = jnp.min(jnp.where(rest == m2, lane, LANE), axis=-1, keepdims=True)
    e2 = jnp.exp(m2 - m1)
    den = 1.0 + e2
    g1 = 1.0 / den
    g2 = e2 / den
    sel = jnp.where(lane == 0, i1.astype(F32),
                    jnp.where(lane == 1, i2.astype(F32),
                              jnp.where(lane == 2, g1, jnp.where(lane == 3, g2, 0.0))))
    sel_ref[...] = sel


def _rmsnorm_router(x, g, router):
    t, d = x.shape
    n_exp = router.shape[1]
    tr = _tile(t, 256)
    wpad = jnp.zeros((d, LANE), F32).at[:, :n_exp].set(router.astype(F32))
    return pl.pallas_call(
        functools.partial(_rmsnorm_router_body, n_exp=n_exp),
        grid=(t // tr,),
        in_specs=[pl.BlockSpec((tr, d), lambda i: (i, 0)), pl.BlockSpec((1, d), lambda i: (0, 0)),
                  pl.BlockSpec((d, LANE), lambda i: (0, 0))],
        out_specs=[pl.BlockSpec((tr, d // LANE, LANE), lambda i: (i, 0, 0)), pl.BlockSpec((tr, LANE), lambda i: (i, 0))],
        out_shape=[jax.ShapeDtypeStruct((t, d // LANE, LANE), BF16), jax.ShapeDtypeStruct((t, LANE), F32)],
        compiler_params=_params("parallel"),
        name="rmsnorm_router",
    )(x, g.reshape(1, d).astype(F32), wpad)


def _mm_body(a_ref, b_ref, *refs, nk, has_res):
    if has_res:
        r_ref, *refs = refs
    o_ref, *scr = refs

    def finish(acc):
        if has_res:
            acc = r_ref[...] + acc
        o_ref[...] = acc.astype(o_ref.dtype)

    part = _dot(a_ref[...], b_ref[...])
    if nk == 1:
        finish(part)
    else:
        k = pl.program_id(2)
        acc_ref = scr[0]

        @pl.when(k == 0)
        def _():
            acc_ref[...] = part

        @pl.when(k > 0)
        def _():
            acc_ref[...] += part

        @pl.when(k == nk - 1)
        def _():
            finish(acc_ref[...])


def _matmul(a, b, *, out_dtype, layer=None, res=None, tm=1024, tn=1024, tk=4096, name="matmul"):
    m, kdim = a.shape
    n = b.shape[-1]
    tm = _tile(m, tm)
    tn = _tile(n, tn, LANE)
    tk = _tile(kdim, tk, LANE)
    nk = kdim // tk
    assert m % tm == 0 and n % tn == 0 and kdim % tk == 0 and b.shape[-2] == kdim
    if layer is None:
        b_spec = pl.BlockSpec((tk, tn), lambda i, j, k: (k, j))
    else:
        b_spec = pl.BlockSpec((None, tk, tn), lambda i, j, k: (layer, k, j))
    in_specs = [pl.BlockSpec((tm, tk), lambda i, j, k: (i, k)), b_spec]
    args = [a, b]
    if res is not None:
        in_specs.append(pl.BlockSpec((tm, tn), lambda i, j, k: (i, j)))
        args.append(res)
    return pl.pallas_call(
        functools.partial(_mm_body, nk=nk, has_res=res is not None),
        grid=(m // tm, n // tn, nk),
        in_specs=in_specs,
        out_specs=pl.BlockSpec((tm, tn), lambda i, j, k: (i, j)),
        out_shape=jax.ShapeDtypeStruct((m, n), out_dtype),
        scratch_shapes=[pltpu.VMEM((tm, tn), F32)] if nk > 1 else [],
        compiler_params=_params("parallel", "parallel", "arbitrary"),
        name=name,
    )(*args)


def _mm_wst_body(a_ref, wt_ref, o_ref, wb_ref):
    @pl.when(pl.program_id(1) == 0)
    def _():
        wb_ref[...] = wt_ref[...].T.astype(BF16)

    o_ref[...] = jnp.dot(a_ref[...].astype(BF16), wb_ref[...], preferred_element_type=F32).astype(o_ref.dtype)


def _matmul_wst(a, wt, *, row0, n, out_dtype, tm=1024, tn=512, name="matmul_wst"):
    m, kdim = a.shape
    tm = _tile(m, tm)
    tn = _tile(n, tn, LANE)
    assert m % tm == 0 and n % tn == 0 and wt.shape[1] == kdim and row0 % 8 == 0
    return pl.pallas_call(
        _mm_wst_body,
        grid=(n // tn, m // tm),
        in_specs=[pl.BlockSpec((tm, kdim), lambda j, i: (i, 0)),
                  pl.BlockSpec((pl.Element(tn), pl.Element(kdim)),
                               lambda j, i: (pl.multiple_of(row0 + j * tn, 8), 0))],
        out_specs=pl.BlockSpec((tm, tn), lambda j, i: (i, j)),
        out_shape=jax.ShapeDtypeStruct((m, n), out_dtype),
        scratch_shapes=[pltpu.VMEM((kdim, tn), BF16)],
        compiler_params=_params("parallel", "arbitrary"),
        name=name,
    )(a, wt)


def _silu_mul(h1, h3):
    return h1 * _sigmoid(h1) * h3


def _glu_body(a_ref, w1_ref, w3_ref, o_ref):
    a = a_ref[...]
    o_ref[...] = _silu_mul(_dot(a, w1_ref[...]), _dot(a, w3_ref[...])).astype(o_ref.dtype)


def _glu(a, w1, w3, *, tm=1024, tn=512, name="glu"):
    m, kdim = a.shape
    n = w1.shape[-1]
    tm = _tile(m, tm)
    tn = _tile(n, tn, LANE)
    assert m % tm == 0 and n % tn == 0
    return pl.pallas_call(
        _glu_body,
        grid=(m // tm, n // tn),
        in_specs=[pl.BlockSpec((tm, kdim), lambda i, j: (i, 0)),
                  pl.BlockSpec((kdim, tn), lambda i, j: (0, j)),
                  pl.BlockSpec((kdim, tn), lambda i, j: (0, j))],
        out_specs=pl.BlockSpec((tm, tn), lambda i, j: (i, j)),
        out_shape=jax.ShapeDtypeStruct((m, n), BF16),
        compiler_params=_params("parallel", "arbitrary"),
        name=name,
    )(a, w1, w3)


def _for_each_sub(n_live, live_fn, dead_fn):
    @pl.when(n_live == MOE_NSUB)
    def _():
        live_fn(slice(None))

    @pl.when(n_live < MOE_NSUB)
    def _():
        for s in range(MOE_NSUB):
            rows = slice(s * MOE_SUB, (s + 1) * MOE_SUB)
            pl.when(s < n_live)(functools.partial(live_fn, rows))
            pl.when(s >= n_live)(functools.partial(dead_fn, rows))


def _expert_glu_body(be_ref, ns_ref, a_ref, w1_ref, w3_ref, o_ref):
    del be_ref

    def live(rows):
        a = a_ref[rows, :]
        h1 = jnp.dot(a, w1_ref[...].astype(BF16), preferred_element_type=F32)
        h3 = jnp.dot(a, w3_ref[...].astype(BF16), preferred_element_type=F32)
        o_ref[rows, :] = _silu_mul(h1, h3).astype(o_ref.dtype)

    def dead(rows):
        o_ref[rows, :] = jnp.zeros((MOE_SUB, o_ref.shape[1]), o_ref.dtype)

    _for_each_sub(ns_ref[pl.program_id(0)], live, dead)


def _expert_glu(xs, w1, w3, layer, block_expert, n_sub, *, tn=512):
    m, kdim = xs.shape
    n = w1.shape[-1]
    tn = _tile(n, tn, LANE)
    nj = n // tn
    assert m % MOE_BLOCK == 0 and n % tn == 0

    def w_map(i, j, be, ns):
        return layer, be[i], 0, jnp.where(ns[i] > 0, j, nj - 1)

    gs = pltpu.PrefetchScalarGridSpec(
        num_scalar_prefetch=2,
        grid=(m // MOE_BLOCK, nj),
        in_specs=[pl.BlockSpec((MOE_BLOCK, kdim), lambda i, j, be, ns: (i, 0), pipeline_mode=pl.Buffered(1)),
                  pl.BlockSpec((None, None, kdim, tn), w_map),
                  pl.BlockSpec((None, None, kdim, tn), w_map)],
        out_specs=pl.BlockSpec((MOE_BLOCK, tn), lambda i, j, be, ns: (i, j)),
    )
    return pl.pallas_call(
        _expert_glu_body, grid_spec=gs, out_shape=jax.ShapeDtypeStruct((m, n), BF16),
        compiler_params=_params("parallel", "arbitrary"), name="expert_glu",
    )(block_expert, n_sub, xs, w1, w3)


def _expert_down_body(be_ref, ns_ref, a_ref, w_ref, o_ref, acc_ref, *, nk):
    del be_ref
    k = pl.program_id(2)

    def live(rows):
        part = jnp.dot(a_ref[rows, :], w_ref[...].astype(BF16), preferred_element_type=F32)

        @pl.when(k == 0)
        def _():
            acc_ref[rows, :] = part

        @pl.when(k > 0)
        def _():
            acc_ref[rows, :] += part

        @pl.when(k == nk - 1)
        def _():
            acc = acc_ref[rows, :]
            o_ref[rows] = acc.reshape((acc.shape[0],) + o_ref.shape[1:])

    def dead(rows):
        @pl.when(k == nk - 1)
        def _():
            o_ref[rows] = jnp.zeros((MOE_SUB,) + o_ref.shape[1:], o_ref.dtype)

    _for_each_sub(ns_ref[pl.program_id(0)], live, dead)


def _expert_down(hmid, w2, layer, block_expert, n_sub, *, tn=1024, tk=2048):
    m, kdim = hmid.shape
    n = w2.shape[-1]
    tn = _tile(n, tn, LANE)
    tk = _tile(kdim, tk, LANE)
    nj, nk = n // tn, kdim // tk
    assert m % MOE_BLOCK == 0 and n % tn == 0 and kdim % tk == 0

    def w_map(i, j, k, be, ns):
        live = ns[i] > 0
        return layer, be[i], jnp.where(live, k, nk - 1), jnp.where(live, j, nj - 1)

    gs = pltpu.PrefetchScalarGridSpec(
        num_scalar_prefetch=2,
        grid=(m // MOE_BLOCK, nj, nk),
        in_specs=[pl.BlockSpec((MOE_BLOCK, tk), lambda i, j, k, be, ns: (i, k)),
                  pl.BlockSpec((None, None, tk, tn), w_map)],
        out_specs=pl.BlockSpec((MOE_BLOCK, tn // LANE, LANE), lambda i, j, k, be, ns: (i, j, 0)),
        scratch_shapes=[pltpu.VMEM((MOE_BLOCK, tn), F32)],
    )
    return pl.pallas_call(
        functools.partial(_expert_down_body, nk=nk), grid_spec=gs,
        out_shape=jax.ShapeDtypeStruct((m, n // LANE, LANE), F32),
        compiler_params=_params("parallel", "parallel", "arbitrary"), name="expert_down",
    )(block_expert, n_sub, hmid, w2)


def _logf_body(a_ref, w_ref, b_ref, o_ref):
    z = _dot(a_ref[...], w_ref[...]) + b_ref[...]
    o_ref[...] = jnp.minimum(z, 0.0) - jnp.log1p(jnp.exp(-jnp.abs(z)))


def _logf(xn, w_f, b_f):
    t, d = xn.shape
    nh = w_f.shape[1]
    wpad = jnp.zeros((d, LANE), F32).at[:, :nh].set(w_f.astype(F32))
    bpad = jnp.zeros((1, LANE), F32).at[0, :nh].set(b_f.astype(F32))
    tr = _tile(t, 1024)
    out = pl.pallas_call(
        _logf_body,
        grid=(t // tr,),
        in_specs=[pl.BlockSpec((tr, d), lambda i: (i, 0)), pl.BlockSpec((d, LANE), lambda i: (0, 0)),
                  pl.BlockSpec((1, LANE), lambda i: (0, 0))],
        out_specs=pl.BlockSpec((tr, LANE), lambda i: (i, 0)),
        out_shape=jax.ShapeDtypeStruct((t, LANE), F32),
        compiler_params=_params("parallel"),
        name="logf",
    )(xn, wpad, bpad)
    return out[:, :nh]


def _cumsum_body(x_ref, o_ref, *, nchunk):
    r = lax.broadcasted_iota(jnp.int32, (LANE, LANE), 0)
    c = lax.broadcasted_iota(jnp.int32, (LANE, LANE), 1)
    tri = (r <= c).astype(BF16)
    rows = x_ref.shape[1]

    def step(i, carry):
        h, m, l = _split3(x_ref[i])
        cs = (jnp.dot(h, tri, preferred_element_type=F32) + jnp.dot(m, tri, preferred_element_type=F32)
              + jnp.dot(l, tri, preferred_element_type=F32)) + carry
        o_ref[i] = cs
        return cs[:, LANE - 1:LANE]

    lax.fori_loop(0, nchunk, step, jnp.zeros((rows, 1), F32))


def _cumsum_time(x):
    b, l, nh = x.shape
    lp = -(-l // LANE) * LANE
    nchunk = lp // LANE
    rows = b * nh
    rp = -(-rows // 8) * 8
    xt = jnp.transpose(x, (0, 2, 1)).reshape(rows, l)
    xt = jnp.pad(xt, ((0, rp - rows), (0, lp - l)))
    xc = jnp.transpose(xt.reshape(rp, nchunk, LANE), (1, 0, 2))
    out = pl.pallas_call(
        functools.partial(_cumsum_body, nchunk=nchunk),
        out_shape=jax.ShapeDtypeStruct((nchunk, rp, LANE), F32),
        compiler_params=pltpu.CompilerParams(vmem_limit_bytes=V7X_VMEM_LIMIT),
        name="cumsum_time",
    )(xc)
    out = jnp.transpose(out, (1, 0, 2)).reshape(rp, lp)[:rows, :l]
    return out.reshape(b, nh, l)


def _fox_update(q, k, v, cq, ck, mask, m_ref, l_ref, acc_ref, hd, dh, scale):
    s = _dot_nt(k, q) * scale
    s = s + cq - ck
    if mask is not None:
        s = jnp.where(mask, s, NEG_INF)
    m_prev = m_ref[hd]
    m_new = jnp.maximum(m_prev, jnp.max(s, axis=0, keepdims=True))
    alpha = jnp.exp(m_prev - m_new)
    p = jnp.exp(s - m_new)
    l_ref[hd] = alpha * l_ref[hd] + jnp.sum(p, axis=0, keepdims=True)
    rows = slice(hd * dh, (hd + 1) * dh)
    acc_ref[rows, :] = alpha * acc_ref[rows, :] + _dot_tn(v, p)
    m_ref[hd] = m_new


def _fox_init(m_ref, l_ref, acc_ref):
    m_ref[...] = jnp.full(m_ref.shape, NEG_INF, F32)
    l_ref[...] = jnp.zeros(l_ref.shape, F32)
    acc_ref[...] = jnp.zeros(acc_ref.shape, F32)


def _fox_finish(o_ref, m_ref, l_ref, acc_ref, nh, dh):
    for hd in range(nh):
        cols = slice(hd * dh, (hd + 1) * dh)
        o_ref[:, cols] = (acc_ref[cols, :] / l_ref[hd]).T.astype(o_ref.dtype)


def _fox_prompt_body(q_ref, k_ref, v_ref, cq_ref, ck_ref, o_ref, m_ref, l_ref, acc_ref, *, nh, dh, tq, nkv):
    qi = pl.program_id(1)
    kj = pl.program_id(2)

    @pl.when(kj == 0)
    def _():
        _fox_init(m_ref, l_ref, acc_ref)

    def update(mask):
        for hd in range(nh):
            cols = slice(hd * dh, (hd + 1) * dh)
            _fox_update(q_ref[:, cols], k_ref[:, cols], v_ref[:, cols], cq_ref[hd], ck_ref[:, hd:hd + 1], mask,
                        m_ref, l_ref, acc_ref, hd, dh, dh ** -0.5)

    @pl.when(kj < qi)
    def _():
        update(None)

    @pl.when(kj == qi)
    def _():
        key_pos = lax.broadcasted_iota(jnp.int32, (tq, tq), 0)
        update(key_pos <= lax.broadcasted_iota(jnp.int32, (tq, tq), 1))

    @pl.when(kj == nkv - 1)
    def _():
        _fox_finish(o_ref, m_ref, l_ref, acc_ref, nh, dh)


def _fox_prompt(h, csum, *, nb, seq, nh, dh, col_q, col_k, col_v):
    fw = nh * dh
    tq = _tile(seq, 512)
    nq = seq // tq
    assert col_q % fw == 0 and col_k % fw == 0 and col_v % fw == 0
    cq = csum.reshape(nb, nh, 1, seq)
    ck = jnp.transpose(csum, (0, 2, 1))
    body = functools.partial(_fox_prompt_body, nh=nh, dh=dh, tq=tq, nkv=nq)
    return pl.pallas_call(
        body,
        grid=(nb, nq, nq),
        in_specs=[
            pl.BlockSpec((tq, fw), lambda b, i, j: (b * nq + i, col_q // fw)),
            pl.BlockSpec((tq, fw), lambda b, i, j: (b * nq + jnp.minimum(i, j), col_k // fw)),
            pl.BlockSpec((tq, fw), lambda b, i, j: (b * nq + jnp.minimum(i, j), col_v // fw)),
            pl.BlockSpec((None, nh, 1, tq), lambda b, i, j: (b, 0, 0, i)),
            pl.BlockSpec((None, tq, nh), lambda b, i, j: (b, jnp.minimum(i, j), 0)),
        ],
        out_specs=pl.BlockSpec((tq, fw), lambda b, i, j: (b * nq + i, 0)),
        out_shape=jax.ShapeDtypeStruct((nb * seq, fw), BF16),
        scratch_shapes=[pltpu.VMEM((nh, 1, tq), F32), pltpu.VMEM((nh, 1, tq), F32), pltpu.VMEM((fw, tq), F32)],
        compiler_params=_params("parallel", "parallel", "arbitrary"),
        name="fox_prompt",
    )(h, h, h, cq, ck)


def _fox_sample_body(q_ref, kn_ref, vn_ref, kp_ref, vp_ref, cq_ref, ckp_ref, ckn_ref, o_ref,
                     m_ref, l_ref, acc_ref, *, nh, dh, lq, npast, tkp):
    kj = pl.program_id(1)
    scale = dh ** -0.5

    @pl.when(kj == 0)
    def _():
        _fox_init(m_ref, l_ref, acc_ref)

    @pl.when(kj < npast)
    def _():
        for hd in range(nh):
            cols = slice(hd * dh, (hd + 1) * dh)
            head_rows = pl.ds(hd, tkp, stride=nh)
            _fox_update(q_ref[:, cols], kp_ref[head_rows, :], vp_ref[head_rows, :], cq_ref[hd],
                        ckp_ref[:, hd:hd + 1], None, m_ref, l_ref, acc_ref, hd, dh, scale)

    @pl.when(kj == npast)
    def _():
        key_pos = lax.broadcasted_iota(jnp.int32, (lq, lq), 0)
        mask = key_pos <= lax.broadcasted_iota(jnp.int32, (lq, lq), 1)
        for hd in range(nh):
            cols = slice(hd * dh, (hd + 1) * dh)
            _fox_update(q_ref[:, cols], kn_ref[:, cols], vn_ref[:, cols], cq_ref[hd], ckn_ref[:, hd:hd + 1], mask,
                        m_ref, l_ref, acc_ref, hd, dh, scale)
        _fox_finish(o_ref, m_ref, l_ref, acc_ref, nh, dh)


def _fox_sample(h, past_k, past_v, layer, csum, *, row0, nb, lq, nh, dh, col_q, col_k, col_v):
    fw = nh * dh
    depth, _, plen = past_k.shape[:3]
    tkp = _tile(plen, 1024)
    npast = plen // tkp
    assert row0 % lq == 0 and col_q % fw == 0 and col_k % fw == 0 and col_v % fw == 0
    rb = row0 // lq
    cq = csum[:, :, plen:].reshape(nb, nh, 1, lq)
    ck = jnp.transpose(csum, (0, 2, 1))
    ckp = ck[:, :plen]
    ckn = ck[:, plen:]
    past_k = past_k.reshape(depth, nb, plen * nh, dh)
    past_v = past_v.reshape(depth, nb, plen * nh, dh)
    body = functools.partial(_fox_sample_body, nh=nh, dh=dh, lq=lq, npast=npast, tkp=tkp)
    pmap = lambda b, j: (layer, b, jnp.minimum(j, npast - 1), 0)
    return pl.pallas_call(
        body,
        grid=(nb, npast + 1),
        in_specs=[
            pl.BlockSpec((lq, fw), lambda b, j: (rb + b, col_q // fw)),
            pl.BlockSpec((lq, fw), lambda b, j: (rb + b, col_k // fw)),
            pl.BlockSpec((lq, fw), lambda b, j: (rb + b, col_v // fw)),
            pl.BlockSpec((None, None, tkp * nh, dh), pmap),
            pl.BlockSpec((None, None, tkp * nh, dh), pmap),
            pl.BlockSpec((None, nh, 1, lq), lambda b, j: (b, 0, 0, 0)),
            pl.BlockSpec((None, tkp, nh), lambda b, j: (b, jnp.minimum(j, npast - 1), 0)),
            pl.BlockSpec((None, lq, nh), lambda b, j: (b, 0, 0)),
        ],
        out_specs=pl.BlockSpec((lq, fw), lambda b, j: (b, 0)),
        out_shape=jax.ShapeDtypeStruct((nb * lq, fw), BF16),
        scratch_shapes=[pltpu.VMEM((nh, 1, lq), F32), pltpu.VMEM((nh, 1, lq), F32), pltpu.VMEM((fw, lq), F32)],
        compiler_params=_params("parallel", "arbitrary"),
        name="fox_sample",
    )(h, h, h, past_k, past_v, cq, ckp, ckn)


def _pool_body(u_ref, prev_ref, hist_ref, w_ref, sc_ref, o_ref, ext_ref, *, tl, gd, start_pos):
    i = pl.program_id(1)
    first = i == 0
    ext_ref[0:HALO, :] = jnp.where(first, hist_ref[...], prev_ref[...])
    ext_ref[HALO:HALO + tl, :] = u_ref[...]
    pos = start_pos + i * tl + lax.broadcasted_iota(jnp.int32, (tl, 1), 0)
    for g, w in enumerate(POOL_WINDOWS):
        cols = slice(g * gd, (g + 1) * gd)
        cur = ext_ref[HALO:HALO + tl, cols]
        win = cur
        for d in range(1, w):
            win = win + ext_ref[HALO - d:HALO - d + tl, cols]
        cnt = jnp.minimum(pos + 1, w).astype(F32)
        pooled = win / cnt - cur
        y = _dot(pooled, w_ref[g]) * sc_ref[:, cols]
        o_ref[:, cols] = y.astype(o_ref.dtype)


def _pool(h, hist, w_pool, pool_scale, *, row0, nb, seq, col_u, start_pos):
    ng, gd, _ = w_pool.shape
    c = ng * gd
    nhist = hist.shape[1]
    assert nhist < HALO and max(POOL_WINDOWS) - 1 <= nhist and ng == len(POOL_WINDOWS)
    tl = _tile(seq, 512, HALO)
    nl = seq // tl
    assert col_u % c == 0 and row0 % tl == 0 and tl % HALO == 0 and seq % HALO == 0
    rb = row0 // tl
    hb = tl // HALO
    hist_p = jnp.concatenate([jnp.zeros((nb, HALO - nhist, c), F32), hist.astype(F32)], axis=1)
    body = functools.partial(_pool_body, tl=tl, gd=gd, start_pos=start_pos)
    return pl.pallas_call(
        body,
        grid=(nb, nl),
        in_specs=[
            pl.BlockSpec((tl, c), lambda b, i: (rb + b * nl + i, col_u // c)),
            pl.BlockSpec((HALO, c), lambda b, i: (jnp.maximum((rb + b * nl + i) * hb - 1, 0), col_u // c)),
            pl.BlockSpec((None, HALO, c), lambda b, i: (b, 0, 0)),
            pl.BlockSpec((ng, gd, gd), lambda b, i: (0, 0, 0)),
            pl.BlockSpec((1, c), lambda b, i: (0, 0)),
        ],
        out_specs=pl.BlockSpec((tl, c), lambda b, i: (b * nl + i, 0)),
        out_shape=jax.ShapeDtypeStruct((nb * seq, c), BF16),
        scratch_shapes=[pltpu.VMEM((HALO + tl, c), F32)],
        compiler_params=_params("parallel", "arbitrary"),
        name="pool",
    )(h, h, hist_p, w_pool.astype(BF16), pool_scale.reshape(1, c).astype(F32))


def _ret_body(q_ref, k_ref, v_ref, g_ref, cos_ref, sin_ref, ld_ref, s0_ref, o_ref, sout_ref, s_ref, *, ch, dk, nchunk):
    n = pl.program_id(2)

    @pl.when(n == 0)
    def _():
        s_ref[...] = s0_ref[...]

    ld = ld_ref[:, 0:1]
    cos_t = cos_ref[...]
    sin_t = sin_ref[...]

    def rope(x):
        return x * cos_t + pltpu.roll(x, dk // 2, axis=1) * sin_t

    q = rope(q_ref[...])
    k = rope(k_ref[...]) * (dk ** -0.5)
    v = v_ref[...]
    ii = lax.broadcasted_iota(jnp.int32, (ch, 1), 0).astype(F32)
    diff = ii - lax.broadcasted_iota(jnp.int32, (1, ch), 1).astype(F32)
    dmask = jnp.where(diff >= 0, jnp.exp(ld * jnp.maximum(diff, 0.0)), 0.0)
    scores = _dot_nt(q, k) * dmask
    o = _dot(scores, v)
    s_prev = s_ref[...]
    o = o + _dot(q * jnp.exp(ld * (ii + 1.0)), s_prev)
    k_dec = k * jnp.exp(ld * (ch - 1.0 - ii))
    s_ref[...] = jnp.exp(ld * ch) * s_prev + _dot_tn(k_dec, v)
    o = o * lax.rsqrt(jnp.mean(o * o, axis=-1, keepdims=True) + EPS)
    gate = g_ref[...]
    o_ref[...] = (o * (gate * _sigmoid(gate))).astype(o_ref.dtype)

    @pl.when(n == nchunk - 1)
    def _():
        sout_ref[...] = s_ref[...]


def _retention(h, state0, *, row0, nb, seq, col_q, col_k, col_v, col_g, start_pos):
    _, nh, dk, dv = state0.shape
    ch = _tile(seq, 256)
    nchunk = seq // ch
    assert row0 % ch == 0 and col_q % dk == 0 and col_k % dk == 0 and col_v % dv == 0 and col_g % dv == 0
    rb = row0 // ch
    half = dk // 2
    inv = ROPE_BASE ** (-jnp.arange(half, dtype=F32) / half)
    ang = (start_pos + jnp.arange(seq, dtype=jnp.int32)).astype(F32)[:, None] * inv[None, :]
    cos_t = jnp.concatenate([jnp.cos(ang), jnp.cos(ang)], axis=1)
    sin_t = jnp.concatenate([-jnp.sin(ang), jnp.sin(ang)], axis=1)
    ld = jnp.log1p(-jnp.exp2(-5.0 - jnp.arange(nh, dtype=F32)))
    ldv = jnp.broadcast_to(ld[:, None, None], (nh, 1, LANE))
    body = functools.partial(_ret_body, ch=ch, dk=dk, nchunk=nchunk)
    rows = lambda b, hd, n: rb + b * nchunk + n
    return pl.pallas_call(
        body,
        grid=(nb, nh, nchunk),
        in_specs=[
            pl.BlockSpec((ch, dk), lambda b, hd, n: (rows(b, hd, n), col_q // dk + hd)),
            pl.BlockSpec((ch, dk), lambda b, hd, n: (rows(b, hd, n), col_k // dk + hd)),
            pl.BlockSpec((ch, dv), lambda b, hd, n: (rows(b, hd, n), col_v // dv + hd)),
            pl.BlockSpec((ch, dv), lambda b, hd, n: (rows(b, hd, n), col_g // dv + hd)),
            pl.BlockSpec((ch, dk), lambda b, hd, n: (n, 0)),
            pl.BlockSpec((ch, dk), lambda b, hd, n: (n, 0)),
            pl.BlockSpec((None, 1, LANE), lambda b, hd, n: (hd, 0, 0)),
            pl.BlockSpec((None, None, dk, dv), lambda b, hd, n: (b, hd, 0, 0)),
        ],
        out_specs=[
            pl.BlockSpec((ch, dv), lambda b, hd, n: (b * nchunk + n, hd)),
            pl.BlockSpec((None, None, dk, dv), lambda b, hd, n: (b, hd, 0, 0)),
        ],
        out_shape=[jax.ShapeDtypeStruct((nb * seq, nh * dv), BF16),
                   jax.ShapeDtypeStruct((nb, nh, dk, dv), F32)],
        scratch_shapes=[pltpu.VMEM((dk, dv), F32)],
        compiler_params=_params("parallel", "parallel", "arbitrary"),
        name="retention",
    )(h, h, h, h, cos_t, sin_t, ldv, state0.astype(F32))


def _merge_body(fo_ref, po_ref, ro_ref, wf_ref, wp_ref, wr_ref, g0_ref, g1_ref, g2_ref, o_ref):
    acc = _sigmoid(g0_ref[...]) * _dot(fo_ref[...], wf_ref[...])
    acc = acc + _sigmoid(g1_ref[...]) * _dot(po_ref[...], wp_ref[...])
    acc = acc + _sigmoid(g2_ref[...]) * _dot(ro_ref[...], wr_ref[...])
    o_ref[...] = acc.astype(o_ref.dtype)


def _merge(fo, po, ro, wf, wp, wr, h, *, col_g, d):
    t = fo.shape[0]
    tm = _tile(t, 1024)
    tn = _tile(math.gcd(d, col_g), 512, LANE)
    gb = col_g // tn
    nd = d // tn
    kf, kp, kr = fo.shape[1], po.shape[1], ro.shape[1]
    return pl.pallas_call(
        _merge_body,
        grid=(t // tm, nd),
        in_specs=[
            pl.BlockSpec((tm, kf), lambda i, j: (i, 0)),
            pl.BlockSpec((tm, kp), lambda i, j: (i, 0)),
            pl.BlockSpec((tm, kr), lambda i, j: (i, 0)),
            pl.BlockSpec((kf, tn), lambda i, j: (0, j)),
            pl.BlockSpec((kp, tn), lambda i, j: (0, j)),
            pl.BlockSpec((kr, tn), lambda i, j: (0, j)),
            pl.BlockSpec((tm, tn), lambda i, j: (i, gb + j)),
            pl.BlockSpec((tm, tn), lambda i, j: (i, gb + nd + j)),
            pl.BlockSpec((tm, tn), lambda i, j: (i, gb + 2 * nd + j)),
        ],
        out_specs=pl.BlockSpec((tm, tn), lambda i, j: (i, j)),
        out_shape=jax.ShapeDtypeStruct((t, d), BF16),
        compiler_params=_params("parallel", "arbitrary"),
        name="merge",
    )(fo, po, ro, wf, wp, wr, h, h, h)


def _xattn_body(q_ref, k_ref, v_ref, o_ref, *, nh, dh):
    scale = dh ** -0.5
    for hd in range(nh):
        cols = slice(hd * dh, (hd + 1) * dh)
        s = _dot_nt(q_ref[:, cols], k_ref[:, cols]) * scale
        p = jnp.exp(s - jnp.max(s, axis=-1, keepdims=True))
        o = _dot(p, v_ref[:, cols]) / jnp.sum(p, axis=-1, keepdims=True)
        o_ref[:, cols] = o.astype(o_ref.dtype)


def _xattn(q, mk, mv, *, row0, nb, seq, nh, dh):
    w = nh * dh
    mt = mk.shape[1]
    tq = _tile(seq, 512)
    nq = seq // tq
    assert row0 % tq == 0
    rb = row0 // tq
    return pl.pallas_call(
        functools.partial(_xattn_body, nh=nh, dh=dh),
        grid=(nb, nq),
        in_specs=[
            pl.BlockSpec((tq, w), lambda b, i: (rb + b * nq + i, 0)),
            pl.BlockSpec((None, mt, w), lambda b, i: (b, 0, 0)),
            pl.BlockSpec((None, mt, w), lambda b, i: (b, 0, 0)),
        ],
        out_specs=pl.BlockSpec((tq, w), lambda b, i: (b * nq + i, 0)),
        out_shape=jax.ShapeDtypeStruct((nb * seq, w), BF16),
        compiler_params=_params("parallel", "arbitrary"),
        name="xattn",
    )(q, mk, mv)


def _rank_body(sel_ref, rk_ref, cnt_ref, carry_ref, *, tb):
    i = pl.program_id(0)

    @pl.when(i == 0)
    def _():
        carry_ref[...] = jnp.zeros_like(carry_ref)

    sel = sel_ref[...]
    lane = lax.broadcasted_iota(jnp.int32, sel.shape, 1)
    lanef = lane.astype(F32)
    e1 = sel[:, 0:1]
    e2 = sel[:, 1:2]
    onehot = jnp.logical_or(lanef == e1, lanef == e2).astype(F32)
    r = lax.broadcasted_iota(jnp.int32, (tb, tb), 0)
    c = lax.broadcasted_iota(jnp.int32, (tb, tb), 1)
    before = (c < r).astype(BF16)
    rank = jnp.dot(before, onehot.astype(BF16), preferred_element_type=F32) + carry_ref[...]
    r1 = jnp.sum(jnp.where(lanef == e1, rank, 0.0), axis=-1, keepdims=True)
    r2 = jnp.sum(jnp.where(lanef == e2, rank, 0.0), axis=-1, keepdims=True)
    rk_ref[...] = jnp.where(lane == 0, r1, jnp.where(lane == 1, r2, 0.0))
    total = carry_ref[...] + jnp.sum(onehot, axis=0, keepdims=True)
    carry_ref[...] = total
    cnt_ref[...] = total


def _expert_ranks(sel):
    t = sel.shape[0]
    tb = _tile(t, 512)
    return pl.pallas_call(
        functools.partial(_rank_body, tb=tb),
        grid=(t // tb,),
        in_specs=[pl.BlockSpec((tb, LANE), lambda i: (i, 0))],
        out_specs=[pl.BlockSpec((tb, LANE), lambda i: (i, 0)), pl.BlockSpec((1, LANE), lambda i: (0, 0))],
        out_shape=[jax.ShapeDtypeStruct((t, LANE), F32), jax.ShapeDtypeStruct((1, LANE), F32)],
        scratch_shapes=[pltpu.VMEM((1, LANE), F32)],
        compiler_params=_params("arbitrary"),
        name="expert_ranks",
    )(sel)


def _row_copy(src, dst, sem, s, d):
    return pltpu.make_async_copy(src.at[s], dst.at[d], sem)


def _dispatch_body(ns_ref, tok_ref, x_ref, o_ref, buf_ref, sem):
    n_live = ns_ref[pl.program_id(0)]
    n_rows = n_live * MOE_SUB

    def issue(r, c):
        _row_copy(x_ref, buf_ref, sem, tok_ref[0, 0, r], r).start()
        return c

    lax.fori_loop(0, n_rows, issue, 0)

    def drain(r, c):
        _row_copy(x_ref, buf_ref, sem, 0, 0).wait()
        return c

    lax.fori_loop(0, n_rows, drain, 0)
    d = o_ref.shape[1]
    for s in range(MOE_NSUB):
        rows = slice(s * MOE_SUB, (s + 1) * MOE_SUB)

        @pl.when(s < n_live)
        def _():
            o_ref[rows, :] = buf_ref[rows].reshape(MOE_SUB, d)

        @pl.when(s >= n_live)
        def _():
            o_ref[rows, :] = jnp.zeros((MOE_SUB, d), o_ref.dtype)


def _dispatch(x3, slot_tok, n_sub):
    t, s, _ = x3.shape
    n_slots = slot_tok.shape[0]
    n_blocks = n_slots // MOE_BLOCK
    gs = pltpu.PrefetchScalarGridSpec(
        num_scalar_prefetch=1,
        grid=(n_blocks,),
        in_specs=[pl.BlockSpec((1, 1, MOE_BLOCK), lambda i, ns: (i, 0, 0), memory_space=pltpu.SMEM),
                  pl.BlockSpec(memory_space=pl.ANY)],
        out_specs=pl.BlockSpec((MOE_BLOCK, s * LANE), lambda i, ns: (i, 0)),
        scratch_shapes=[pltpu.VMEM((MOE_BLOCK, s, LANE), x3.dtype), pltpu.SemaphoreType.DMA(())],
    )
    return pl.pallas_call(
        _dispatch_body, grid_spec=gs, out_shape=jax.ShapeDtypeStruct((n_slots, s * LANE), x3.dtype),
        compiler_params=_params("arbitrary"), name="moe_dispatch",
    )(n_sub, slot_tok.reshape(n_blocks, 1, MOE_BLOCK), x3)


def _combine_body(dest_ref, g0_ref, g1_ref, x_ref, y_ref, o_ref, r0_ref, r1_ref, sem, *, tb):
    bufs = (r0_ref, r1_ref)

    def issue(r, c):
        for k in range(TOP_K):
            _row_copy(y_ref, bufs[k], sem, dest_ref[0, 0, TOP_K * r + k], r).start()
        return c

    lax.fori_loop(0, tb, issue, 0)

    def drain(r, c):
        _row_copy(y_ref, r0_ref, sem, 0, 0).wait()
        return c

    lax.fori_loop(0, TOP_K * tb, drain, 0)
    shape = x_ref.shape
    y = r0_ref[...].reshape(shape) * g0_ref[...] + r1_ref[...].reshape(shape) * g1_ref[...]
    o_ref[...] = x_ref[...] + y


def _combine(x, yb3, dest, sel):
    t, d = x.shape
    tb = _tile(t, 256)
    s = d // LANE
    dest3 = dest.reshape(t // tb, 1, TOP_K * tb)
    row = pl.BlockSpec((tb, d), lambda i: (i, 0))
    gate = pl.BlockSpec((tb, 1), lambda i: (i, 0))
    return pl.pallas_call(
        functools.partial(_combine_body, tb=tb),
        grid=(t // tb,),
        in_specs=[pl.BlockSpec((1, 1, TOP_K * tb), lambda i: (i, 0, 0), memory_space=pltpu.SMEM),
                  gate, gate, row, pl.BlockSpec(memory_space=pl.ANY)],
        out_specs=row,
        out_shape=jax.ShapeDtypeStruct((t, d), F32),
        scratch_shapes=[pltpu.VMEM((tb, s, LANE), F32), pltpu.VMEM((tb, s, LANE), F32),
                        pltpu.SemaphoreType.DMA(())],
        compiler_params=_params("arbitrary"),
        name="moe_combine",
    )(dest3, sel[:, 2:3], sel[:, 3:4], x, yb3)


def _moe_ffn(x, g, router, w1, w3, w2, layer):
    t, d = x.shape
    n_exp = router.shape[1]
    xn3, sel = _rmsnorm_router(x, g, router)
    rk, cnt = _expert_ranks(sel)
    counts = cnt[0, :n_exp].astype(jnp.int32)
    nblk = (counts + MOE_BLOCK - 1) // MOE_BLOCK
    nsub = (counts + MOE_SUB - 1) // MOE_SUB
    q = nsub // jnp.maximum(nblk, 1)
    rem = nsub - q * nblk
    region = nblk * MOE_BLOCK
    region_end = jnp.cumsum(region)
    region_start = region_end - region
    n_blocks = -(-(t * TOP_K) // MOE_BLOCK) + n_exp
    n_slots = n_blocks * MOE_BLOCK
    eid = sel[:, :TOP_K].astype(jnp.int32)
    rank = rk[:, :TOP_K].astype(jnp.int32)
    q_t, rem_t = q[eid], rem[eid]
    u = rank // MOE_SUB
    big = rem_t * (q_t + 1)
    blk = jnp.where(u < big, u // (q_t + 1), rem_t + (u - big) // jnp.maximum(q_t, 1))
    first_row = MOE_SUB * (blk * q_t + jnp.minimum(blk, rem_t))
    dest = (region_start[eid] + blk * MOE_BLOCK + (rank - first_row)).astype(jnp.int32)
    block_start = jnp.arange(n_blocks, dtype=jnp.int32) * MOE_BLOCK
    block_exp = jnp.minimum(jnp.sum(block_start[:, None] >= region_end[None, :], axis=1), n_exp - 1).astype(jnp.int32)
    used = block_start < region_end[-1]
    blk_in_region = (block_start - region_start[block_exp]) // MOE_BLOCK
    n_sub = jnp.where(used, q[block_exp] + (blk_in_region < rem[block_exp]), 0).astype(jnp.int32)
    last_used = jnp.maximum(region_end[-1] // MOE_BLOCK - 1, 0)
    block_exp = jnp.where(used, block_exp, block_exp[last_used]).astype(jnp.int32)
    tok = jnp.repeat(jnp.arange(t, dtype=jnp.int32), TOP_K)
    slot_tok = jnp.zeros((n_slots,), jnp.int32).at[dest.reshape(-1)].set(tok)
    xs = _dispatch(xn3, slot_tok, n_sub)
    hmid = _expert_glu(xs, w1, w3, layer, block_exp, n_sub)
    yb3 = _expert_down(hmid, w2, layer, block_exp, n_sub)
    return _combine(x, yb3, dest, sel)


def _pad_cols(w, mult):
    n = w.shape[-1]
    return jnp.pad(w, [(0, 0)] * (w.ndim - 1) + [(0, -(-n // mult) * mult - n)])


def _pad_rows(w, mult):
    n = w.shape[-2]
    return jnp.pad(w, [(0, 0)] * (w.ndim - 2) + [(0, -(-n // mult) * mult - n), (0, 0)])


def kernel(x_prompt, x_sample, mem_prompt, cache_fox_k, cache_fox_v, cache_fox_logf, state_pool, state_ret, cache_mem_k, cache_mem_v, norm_mix_g, w_in, b_forget, w_pool, pool_scale, w_br_fox, w_br_pool, w_br_ret, w_out, norm_x_g, norm_mem_g, w_xq, w_xk, w_xv, w_xo, norm_ffn_g, ffn_w1, ffn_w3, ffn_w2, moe_router, moe_w1, moe_w3, moe_w2, norm_final_g):
    bp, sp, d = x_prompt.shape
    bs, ss, _ = x_sample.shape
    depth = w_in.shape[0]
    past = cache_fox_k.shape[2]
    nh_f, dh_f = cache_fox_k.shape[3], cache_fox_k.shape[4]
    fw = nh_f * dh_f
    nhist, pw = state_pool.shape[2], state_pool.shape[3]
    nh_r, dk_r, dv_r = state_ret.shape[2], state_ret.shape[3], state_ret.shape[4]
    rqk, rvw = nh_r * dk_r, nh_r * dv_r
    mt, nh_m, dh_m = cache_mem_k.shape[2], cache_mem_k.shape[3], cache_mem_k.shape[4]
    mw = nh_m * dh_m
    tp, ts = bp * sp, bs * ss
    assert sp >= nhist and ss >= nhist

    c_fq, c_fk, c_fv = 0, fw, 2 * fw
    c_pu = 0
    c_rq = c_pu + pw
    c_rk = c_rq + rqk
    c_rv = c_rk + rqk
    c_rg = c_rv + rvw
    c_g = c_rg + rvw
    in_width = w_in.shape[2]
    assert in_width == 3 * fw + nh_f + c_g + 3 * d and in_width % 8 == 0 and (3 * fw + nh_f) % 8 == 0
    w_in_t = jnp.swapaxes(w_in, 1, 2).reshape(depth * in_width, d)

    x = jnp.concatenate([x_prompt.reshape(tp, d), x_sample.reshape(ts, d)], axis=0).astype(F32)
    outs = {k: [] for k in ("pk", "pv", "plf", "pp", "pr", "pmk", "pmv", "sk", "sv", "slf", "spool", "sr")}

    for l in range(depth):
        w_flog = w_in_t[l * in_width + 3 * fw:l * in_width + 3 * fw + nh_f].T

        xn = _rmsnorm(x, norm_mix_g[l], BF16)
        ha = _matmul_wst(xn, w_in_t, row0=l * in_width, n=3 * fw, out_dtype=F32, name="proj_in_fox")
        hb = _matmul_wst(xn, w_in_t, row0=l * in_width + 3 * fw + nh_f, n=c_g + 3 * d, out_dtype=F32, name="proj_in")
        logf = _logf(xn, w_flog, b_forget[l])
        logf_p = logf[:tp].reshape(bp, sp, nh_f)
        logf_s = logf[tp:].reshape(bs, ss, nh_f)
        cs_p = _cumsum_time(logf_p)
        cs_s = _cumsum_time(jnp.concatenate([cache_fox_logf[l].astype(F32), logf_s], axis=1))
        fo_p = _fox_prompt(ha, cs_p, nb=bp, seq=sp, nh=nh_f, dh=dh_f, col_q=c_fq, col_k=c_fk, col_v=c_fv)
        fo_s = _fox_sample(ha, cache_fox_k, cache_fox_v, l, cs_s,
                           row0=tp, nb=bs, lq=ss, nh=nh_f, dh=dh_f, col_q=c_fq, col_k=c_fk, col_v=c_fv)
        fo = jnp.concatenate([fo_p, fo_s], axis=0)

        po_p = _pool(hb, jnp.zeros((bp, nhist, pw), F32), w_pool[l], pool_scale[l],
                     row0=0, nb=bp, seq=sp, col_u=c_pu, start_pos=0)
        po_s = _pool(hb, state_pool[l], w_pool[l], pool_scale[l],
                     row0=tp, nb=bs, seq=ss, col_u=c_pu, start_pos=past)
        po = jnp.concatenate([po_p, po_s], axis=0)

        ro_p, rs_p = _retention(hb, jnp.zeros((bp, nh_r, dk_r, dv_r), F32), row0=0, nb=bp, seq=sp,
                                col_q=c_rq, col_k=c_rk, col_v=c_rv, col_g=c_rg, start_pos=0)
        ro_s, rs_s = _retention(hb, state_ret[l], row0=tp, nb=bs, seq=ss,
                                col_q=c_rq, col_k=c_rk, col_v=c_rv, col_g=c_rg, start_pos=past)
        ro = jnp.concatenate([ro_p, ro_s], axis=0)

        merged = _merge(fo, po, ro, w_br_fox[l].astype(BF16), w_br_pool[l].astype(BF16),
                        w_br_ret[l].astype(BF16), hb, col_g=c_g, d=d)
        x = _matmul(merged, w_out, layer=l, out_dtype=F32, res=x, tn=512, name="proj_out")

        mn = _rmsnorm(mem_prompt.reshape(bp * mt, d).astype(F32), norm_mem_g[l], BF16)
        mk = _matmul(mn, w_xk, layer=l, out_dtype=F32, tn=512, name="mem_k")
        mv = _matmul(mn, w_xv, layer=l, out_dtype=F32, tn=512, name="mem_v")
        xn = _rmsnorm(x, norm_x_g[l], BF16)
        q = _matmul(xn, w_xq, layer=l, out_dtype=BF16, tn=512, name="xattn_q")
        xo_p = _xattn(q, mk.reshape(bp, mt, mw), mv.reshape(bp, mt, mw), row0=0, nb=bp, seq=sp, nh=nh_m, dh=dh_m)
        xo_s = _xattn(q, cache_mem_k[l].reshape(bs, mt, mw).astype(F32), cache_mem_v[l].reshape(bs, mt, mw).astype(F32),
                      row0=tp, nb=bs, seq=ss, nh=nh_m, dh=dh_m)
        x = _matmul(jnp.concatenate([xo_p, xo_s], axis=0), w_xo, layer=l, out_dtype=F32, res=x, tn=512,
                    name="xattn_out")

        j = l // 2
        if l % 2 == 0:
            xn = _rmsnorm(x, norm_ffn_g[l], BF16)
            w1 = _pad_cols(ffn_w1[j], 1024).astype(BF16)
            w3 = _pad_cols(ffn_w3[j], 1024).astype(BF16)
            w2 = _pad_rows(ffn_w2[j], 1024).astype(BF16)
            hmid = _glu(xn, w1, w3, name="ffn_glu")
            x = _matmul(hmid, w2, out_dtype=F32, res=x, tk=w2.shape[0] // 4, name="ffn_down")
        else:
            x = _moe_ffn(x, norm_ffn_g[l], moe_router[j], moe_w1, moe_w3, moe_w2, j)

        outs["pk"].append(ha[:tp, c_fk:c_fk + fw].reshape(bp, sp, nh_f, dh_f))
        outs["pv"].append(ha[:tp, c_fv:c_fv + fw].reshape(bp, sp, nh_f, dh_f))
        outs["plf"].append(logf_p)
        outs["pp"].append(hb[:tp, c_pu:c_pu + pw].reshape(bp, sp, pw)[:, sp - nhist:])
        outs["pr"].append(rs_p)
        outs["pmk"].append(mk.reshape(bp, mt, nh_m, dh_m))
        outs["pmv"].append(mv.reshape(bp, mt, nh_m, dh_m))
        outs["sk"].append(ha[tp:, c_fk:c_fk + fw].reshape(bs, ss, nh_f, dh_f))
        outs["sv"].append(ha[tp:, c_fv:c_fv + fw].reshape(bs, ss, nh_f, dh_f))
        outs["slf"].append(logf_s)
        outs["spool"].append(hb[tp:, c_pu:c_pu + pw].reshape(bs, ss, pw)[:, ss - nhist:])
        outs["sr"].append(rs_s)

    y_p = _rmsnorm(x, norm_final_g, F32, row0=0, rows=tp)
    y_s = _rmsnorm(x, norm_final_g, F32, row0=tp, rows=ts)
    st = {k: jnp.stack(v) for k, v in outs.items()}
    return (y_p.reshape(bp, sp, d), y_s.reshape(bs, ss, d),
            st["pk"], st["pv"], st["plf"], st["pp"], st["pr"], st["pmk"], st["pmv"],
            st["sk"], st["sv"], st["slf"], st["spool"], st["sr"])
```
